```python
import jax, jax.numpy as jnp
from jax import lax
import numpy as np

D_MODEL = 4096
BATCH = 16
SEQ = 256
DEPTH = 2
DEC_BATCH = 2
DEC_SEQ = 2048
PAST_LEN = 256

GRID_W = 64
N_MIXERS = 2
N_LAYERS_A = (DEPTH + 1) // 2
N_LAYERS_B = DEPTH // 2
D_FF = 4 * D_MODEL
N_MOD = 6
EPS = 1e-6

MLSTM_HEADS = 8
MLSTM_DK = D_MODEL // (2 * MLSTM_HEADS)
MLSTM_DV = D_MODEL // MLSTM_HEADS
MLSTM_QK_W = MLSTM_HEADS * MLSTM_DK
MLSTM_V_W = MLSTM_HEADS * MLSTM_DV
MLSTM_IN_W = 2 * MLSTM_QK_W + 2 * MLSTM_V_W + 4 * MLSTM_HEADS
MLSTM_CHUNK = 64

MLA_HEADS = 32
MLA_NOPE = 128
MLA_ROPE = 64
MLA_V = 128
MLA_QK_DIM = MLA_NOPE + MLA_ROPE
MLA_Q_LORA = D_MODEL // 4
MLA_KV_LORA = D_MODEL // 8
MLA_IN_W = MLA_Q_LORA + MLA_KV_LORA + MLA_ROPE
ROPE_AXIS_DIM = MLA_ROPE // 2
ROPE_BASE = 10000.0
Q_BLOCK = 128

kernel_name = 'hybrid_mlstm_mla_diffusion_step'


def rms_norm(x, w):
    xf = x.astype(jnp.float32)
    y = xf * lax.rsqrt(jnp.mean(xf * xf, axis=-1, keepdims=True) + EPS)
    return (y * w.astype(jnp.float32)).astype(x.dtype)


def modulation(cond, w, b):
    mod = jax.nn.silu(cond) @ w + b
    return jnp.split(mod[..., None, :], N_MOD, axis=-1)


def modulate(h, shift, scale):
    return h * (1.0 + scale) + shift


def sqrelu_mlp(x, w1, w2):
    return jnp.square(jax.nn.relu(x @ w1)) @ w2


def axial_rope(x):
    n_tok = x.shape[1]
    rows = n_tok // GRID_W
    row = jnp.repeat(jnp.arange(rows, dtype=jnp.float32), GRID_W)
    col = jnp.tile(jnp.arange(GRID_W, dtype=jnp.float32), rows)
    inv_freq = ROPE_BASE ** (-jnp.arange(0, ROPE_AXIS_DIM, 2, dtype=jnp.float32) / ROPE_AXIS_DIM)
    half = ROPE_AXIS_DIM // 2
    bshape = (n_tok,) + (1,) * (x.ndim - 3) + (half,)
    xf = x.astype(jnp.float32)

    def rot(xa, pos):
        ang = (pos[:, None] * inv_freq).reshape(bshape)
        cos, sin = jnp.cos(ang), jnp.sin(ang)
        x1, x2 = xa[..., :half], xa[..., half:]
        return jnp.concatenate([x1 * cos - x2 * sin, x2 * cos + x1 * sin], axis=-1)

    out = jnp.concatenate([rot(xf[..., :ROPE_AXIS_DIM], row), rot(xf[..., ROPE_AXIS_DIM:], col)], axis=-1)
    return out.astype(x.dtype)


def block_attention(q, k, v):
    B, T, H, Dk = q.shape
    nb = T // Q_BLOCK
    scale = Dk ** -0.5
    qb = q.reshape(B, nb, Q_BLOCK, H, Dk).transpose(1, 0, 2, 3, 4)

    def one_block(qi):
        s = jnp.einsum('bqhd,bkhd->bhqk', qi, k).astype(jnp.float32) * scale
        p = jax.nn.softmax(s, axis=-1).astype(v.dtype)
        return jnp.einsum('bhqk,bkhd->bqhd', p, v)

    out = lax.map(one_block, qb)
    return out.transpose(1, 0, 2, 3, 4).reshape(B, T, H, v.shape[-1])


def mlstm_chunkwise(q, k, v, log_i, log_f, C0, n0, m0):
    B, H, T, _ = q.shape
    nc = T // MLSTM_CHUNK

    def to_chunks(a):
        a = a.reshape(a.shape[:2] + (nc, MLSTM_CHUNK) + a.shape[3:])
        return jnp.moveaxis(a, 2, 0)

    lower = jnp.tril(jnp.ones((MLSTM_CHUNK, MLSTM_CHUNK), dtype=bool))

    def step(carry, xs):
        C, n, m = carry
        qc, kc, vc, ic, fc = xs
        b = jnp.cumsum(fc, axis=-1)
        d = jnp.where(lower, b[..., :, None] - b[..., None, :] + ic[..., None, :], -jnp.inf)
        m_inter = b + m[..., None]
        m_t = jnp.maximum(m_inter, jnp.max(d, axis=-1))
        s = jnp.einsum('bhtd,bhsd->bhts', qc, kc) * jnp.exp(d - m_t[..., None])
        a_inter = jnp.exp(m_inter - m_t)
        num = jnp.einsum('bhts,bhsv->bhtv', s, vc) + a_inter[..., None] * jnp.einsum('bhtd,bhdv->bhtv', qc, C)
        den = jnp.sum(s, axis=-1) + a_inter * jnp.einsum('bhtd,bhd->bht', qc, n)
        h = num / jnp.maximum(jnp.abs(den), jnp.exp(-m_t))[..., None]
        b_last = b[..., -1]
        w = b_last[..., None] - b + ic
        m_new = jnp.maximum(b_last + m, jnp.max(w, axis=-1))
        decay = jnp.exp(b_last + m - m_new)
        wk = jnp.exp(w - m_new[..., None])[..., None] * kc
        C_new = decay[..., None, None] * C + jnp.einsum('bhsd,bhsv->bhdv', wk, vc)
        n_new = decay[..., None] * n + jnp.sum(wk, axis=2)
        return (C_new, n_new, m_new), h

    (C, n, m), h = lax.scan(step, (C0, n0, m0), (to_chunks(q), to_chunks(k), to_chunks(v), to_chunks(log_i), to_chunks(log_f)))
    h = jnp.moveaxis(h, 0, 2).reshape(B, H, T, v.shape[-1])
    return h, C, n, m


def mlstm_mixer(x, C0, n0, m0, w_in, gate_b, hnorm_w, w_out):
    B, T, _ = x.shape
    q, k, v, o, g = jnp.split(x @ w_in, [MLSTM_QK_W, 2 * MLSTM_QK_W, 2 * MLSTM_QK_W + MLSTM_V_W, 2 * MLSTM_QK_W + 2 * MLSTM_V_W], axis=-1)

    def heads(a, d):
        return a.reshape(B, T, MLSTM_HEADS, d).transpose(0, 2, 1, 3).astype(jnp.float32)

    q = heads(q, MLSTM_DK) * (MLSTM_DK ** -0.5)
    k = heads(k, MLSTM_DK)
    v = heads(v, MLSTM_DV)
    g = (g + gate_b).astype(jnp.float32).reshape(B, T, 4, MLSTM_HEADS).transpose(2, 0, 3, 1)
    log_i_f, log_f_f = g[0], jax.nn.log_sigmoid(g[1])
    log_i_b, log_f_b = g[2], jax.nn.log_sigmoid(g[3])
    C0 = C0.astype(jnp.float32)
    n0 = n0.astype(jnp.float32)
    m0 = m0.astype(jnp.float32)
    h_f, C_f, n_f, m_f = mlstm_chunkwise(q, k, v, log_i_f, log_f_f, C0[:, 0], n0[:, 0], m0[:, 0])
    rev = lambda a: jnp.flip(a, axis=2)
    h_b, C_b, n_b, m_b = mlstm_chunkwise(rev(q), rev(k), rev(v), rev(log_i_b), rev(log_f_b), C0[:, 1], n0[:, 1], m0[:, 1])
    h = h_f + rev(h_b)
    h = rms_norm(h.transpose(0, 2, 1, 3), hnorm_w)
    y = (h.reshape(B, T, MLSTM_V_W).astype(x.dtype) * jax.nn.sigmoid(o)) @ w_out
    state = (jnp.stack([C_f, C_b], axis=1), jnp.stack([n_f, n_b], axis=1), jnp.stack([m_f, m_b], axis=1))
    return y, state


def qk_head_norm(a, w):
    return jnp.concatenate([rms_norm(a[..., :MLA_NOPE], w[:MLA_NOPE]), rms_norm(a[..., MLA_NOPE:], w[MLA_NOPE:])], axis=-1)


def mla_queries_latents(x, w_in, qnorm_w, kvnorm_w, w_uq, q_norm_w, k_norm_w):
    B, T, _ = x.shape
    cq, ckv, kr = jnp.split(x @ w_in, [MLA_Q_LORA, MLA_Q_LORA + MLA_KV_LORA], axis=-1)
    q = (rms_norm(cq, qnorm_w) @ w_uq).reshape(B, T, MLA_HEADS, MLA_QK_DIM)
    q = qk_head_norm(q, q_norm_w)
    return q, rms_norm(ckv, kvnorm_w), rms_norm(kr, k_norm_w[MLA_NOPE:])


def mla_keys_values(ckv, kr, w_ukv, k_norm_w):
    B, S, _ = ckv.shape
    kv = (ckv @ w_ukv).reshape(B, S, MLA_HEADS, MLA_NOPE + MLA_V)
    k_nope = rms_norm(kv[..., :MLA_NOPE], k_norm_w[:MLA_NOPE])
    k_rope = jnp.broadcast_to(kr[:, :, None, :], (B, S, MLA_HEADS, MLA_ROPE)).astype(k_nope.dtype)
    return jnp.concatenate([k_nope, k_rope], axis=-1), kv[..., MLA_NOPE:]


def mla_context(x, w_in, qnorm_w, kvnorm_w, w_uq, w_ukv, q_norm_w, k_norm_w, w_out):
    B, T, _ = x.shape
    q, ckv, kr = mla_queries_latents(x, w_in, qnorm_w, kvnorm_w, w_uq, q_norm_w, k_norm_w)
    k, v = mla_keys_values(ckv, kr, w_ukv, k_norm_w)
    o = block_attention(q, k, v)
    return o.reshape(B, T, MLA_HEADS * MLA_V) @ w_out, ckv, kr


def mla_latent(x, ckv_ctx, kr_ctx, w_in, qnorm_w, kvnorm_w, w_uq, w_ukv, q_norm_w, k_norm_w, w_out):
    B, T, _ = x.shape
    q, ckv, kr = mla_queries_latents(x, w_in, qnorm_w, kvnorm_w, w_uq, q_norm_w, k_norm_w)
    q = jnp.concatenate([q[..., :MLA_NOPE], axial_rope(q[..., MLA_NOPE:])], axis=-1)
    k_lat, v_lat = mla_keys_values(ckv, axial_rope(kr), w_ukv, k_norm_w)
    k_ctx, v_ctx = mla_keys_values(ckv_ctx.astype(ckv.dtype), kr_ctx, w_ukv, k_norm_w)
    k = jnp.concatenate([k_ctx, k_lat], axis=1)
    v = jnp.concatenate([v_ctx, v_lat], axis=1)
    o = block_attention(q, k, v)
    return o.reshape(B, T, MLA_HEADS * MLA_V) @ w_out


def setup_inputs(seed: int = 0) -> dict:
    key = jax.random.key(seed)
    ks = iter(jax.random.split(key, 28))

    def nrm(shape, scale=1.0):
        return scale * jax.random.normal(next(ks), shape, jnp.float32)

    def gain(shape):
        return 1.0 + nrm(shape, 0.05)

    D = D_MODEL
    gate_base = jnp.repeat(jnp.array([-1.0, 3.0, -1.0, 3.0], jnp.float32), MLSTM_HEADS)[None, :]
    return {
        'x_prompt': nrm((BATCH, SEQ, D)),
        'x_sample': nrm((DEC_BATCH, DEC_SEQ, D)),
        'state_mlstm_C': nrm((DEC_BATCH, N_LAYERS_A, 2, MLSTM_HEADS, MLSTM_DK, MLSTM_DV), 0.1),
        'state_mlstm_n': nrm((DEC_BATCH, N_LAYERS_A, 2, MLSTM_HEADS, MLSTM_DK), 0.1),
        'state_mlstm_m': nrm((DEC_BATCH, N_LAYERS_A, 2, MLSTM_HEADS), 0.5),
        'cache_mla_ckv': nrm((DEC_BATCH, N_LAYERS_B, PAST_LEN, MLA_KV_LORA)),
        'cache_mla_krope': nrm((DEC_BATCH, N_LAYERS_B, PAST_LEN, MLA_ROPE)),
        'c': nrm((DEC_BATCH, D)),
        'c_ctx': nrm((D,)),
        'norm1_w': gain((DEPTH, D)),
        'norm2_w': gain((DEPTH, D)),
        'mod_w': nrm((DEPTH, D, N_MOD * D), 0.5 * D ** -0.5),
        'mod_b': nrm((DEPTH, N_MOD * D), 0.02),
        'mlp_w1': nrm((DEPTH, D, D_FF), D ** -0.5),
        'mlp_w2': nrm((DEPTH, D_FF, D), D_FF ** -0.5),
        'mlstm_w_in': nrm((N_LAYERS_A, D, MLSTM_IN_W), D ** -0.5),
        'mlstm_gate_b': gate_base + nrm((N_LAYERS_A, 4 * MLSTM_HEADS), 0.5),
        'mlstm_hnorm_w': gain((N_LAYERS_A, MLSTM_HEADS, MLSTM_DV)),
        'mlstm_w_out': nrm((N_LAYERS_A, MLSTM_V_W, D), MLSTM_V_W ** -0.5),
        'mla_w_in': nrm((N_LAYERS_B, D, MLA_IN_W), D ** -0.5),
        'mla_qnorm_w': gain((N_LAYERS_B, MLA_Q_LORA)),
        'mla_kvnorm_w': gain((N_LAYERS_B, MLA_KV_LORA)),
        'mla_w_uq': nrm((N_LAYERS_B, MLA_Q_LORA, MLA_HEADS * MLA_QK_DIM), MLA_Q_LORA ** -0.5),
        'mla_w_ukv': nrm((N_LAYERS_B, MLA_KV_LORA, MLA_HEADS * (MLA_NOPE + MLA_V)), MLA_KV_LORA ** -0.5),
        'mla_q_norm_w': gain((N_LAYERS_B, MLA_QK_DIM)),
        'mla_k_norm_w': gain((N_LAYERS_B, MLA_QK_DIM)),
        'mla_w_out': nrm((N_LAYERS_B, MLA_HEADS * MLA_V, D), (MLA_HEADS * MLA_V) ** -0.5),
    }


def reference(x_prompt, x_sample, state_mlstm_C, state_mlstm_n, state_mlstm_m, cache_mla_ckv, cache_mla_krope, c, c_ctx,
              norm1_w, norm2_w, mod_w, mod_b, mlp_w1, mlp_w2,
              mlstm_w_in, mlstm_gate_b, mlstm_hnorm_w, mlstm_w_out,
              mla_w_in, mla_qnorm_w, mla_kvnorm_w, mla_w_uq, mla_w_ukv, mla_q_norm_w, mla_k_norm_w, mla_w_out):
    xp, xs = x_prompt, x_sample
    bp = xp.shape[0]
    new_C, new_n, new_m, new_ckv, new_kr = [], [], [], [], []
    for l in range(DEPTH):
        j = l // N_MIXERS
        mp = modulation(c_ctx, mod_w[l], mod_b[l])
        ms = modulation(c, mod_w[l], mod_b[l])
        hp = modulate(rms_norm(xp, norm1_w[l]), mp[0], mp[1])
        hs = modulate(rms_norm(xs, norm1_w[l]), ms[0], ms[1])
        if l % N_MIXERS == 0:
            wa = (mlstm_w_in[j], mlstm_gate_b[j], mlstm_hnorm_w[j], mlstm_w_out[j])
            zC = jnp.zeros((bp, 2, MLSTM_HEADS, MLSTM_DK, MLSTM_DV), jnp.float32)
            zn = jnp.zeros((bp, 2, MLSTM_HEADS, MLSTM_DK), jnp.float32)
            zm = jnp.zeros((bp, 2, MLSTM_HEADS), jnp.float32)
            yp, (C_ctx, n_ctx, m_ctx) = mlstm_mixer(hp, zC, zn, zm, *wa)
            ys, _ = mlstm_mixer(hs, state_mlstm_C[:, j], state_mlstm_n[:, j], state_mlstm_m[:, j], *wa)
            new_C.append(C_ctx)
            new_n.append(n_ctx)
            new_m.append(m_ctx)
        else:
            wb = (mla_w_in[j], mla_qnorm_w[j], mla_kvnorm_w[j], mla_w_uq[j], mla_w_ukv[j], mla_q_norm_w[j], mla_k_norm_w[j], mla_w_out[j])
            yp, ckv_ctx, kr_ctx = mla_context(hp, *wb)
            ys = mla_latent(hs, cache_mla_ckv[:, j], cache_mla_krope[:, j], *wb)
            new_ckv.append(ckv_ctx)
            new_kr.append(kr_ctx)
        xp = xp + mp[2] * yp
        xs = xs + ms[2] * ys
        xp = xp + mp[5] * sqrelu_mlp(modulate(rms_norm(xp, norm2_w[l]), mp[3], mp[4]), mlp_w1[l], mlp_w2[l])
        xs = xs + ms[5] * sqrelu_mlp(modulate(rms_norm(xs, norm2_w[l]), ms[3], ms[4]), mlp_w1[l], mlp_w2[l])
    new_mlstm_C = jnp.stack(new_C, axis=1)
    new_mlstm_n = jnp.stack(new_n, axis=1)
    new_mlstm_m = jnp.stack(new_m, axis=1)
    new_mla_ckv = jnp.stack(new_ckv, axis=1)
    new_mla_krope = jnp.stack(new_kr, axis=1)
    return (xp, xs, new_mlstm_C, new_mlstm_n, new_mlstm_m, new_mla_ckv, new_mla_krope)
```

```python
import functools

import jax
import jax.numpy as jnp
import numpy as np
from jax import lax
from jax.experimental import pallas as pl
from jax.experimental.pallas import tpu as pltpu

F32 = jnp.float32
BF16 = jnp.bfloat16

D_MODEL = 4096
D_FF = 4 * D_MODEL
N_MOD = 6
EPS = 1e-6
GRID_W = 64

MLSTM_HEADS = 8
MLSTM_DK = 256
MLSTM_DV = 512
MLSTM_QK_W = MLSTM_HEADS * MLSTM_DK
MLSTM_V_W = MLSTM_HEADS * MLSTM_DV
MLSTM_N_GATES = 4 * MLSTM_HEADS
MLSTM_CHUNK = 256

MLA_HEADS = 32
MLA_NOPE = 128
MLA_ROPE = 64
MLA_V = 128
MLA_QK_DIM = MLA_NOPE + MLA_ROPE
MLA_Q_LORA = 1024
MLA_KV_LORA = 512
MLA_IN_PAD = 2048
MLA_HEAD_PAD = 256
ROPE_AXIS_DIM = MLA_ROPE // 2
ROPE_HALF = ROPE_AXIS_DIM // 2
ROPE_BASE = 10000.0

LANES = 128
SUBLANES = 8
MOD_ROWS = SUBLANES
VMEM_LIMIT_BIG = 58 * 1024 * 1024
VMEM_LIMIT_MED = 40 * 1024 * 1024

NT_DIMS = (((1,), (1,)), ((), ()))
TN_DIMS = (((0,), (0,)), ((), ()))


def _params(semantics, vmem=VMEM_LIMIT_MED):
    return pltpu.CompilerParams(dimension_semantics=semantics, vmem_limit_bytes=vmem)


def _mod_row(tile, tm, rows_per_mod, mod_base):
    return mod_base + (tile * tm) // rows_per_mod


def _mod_kernel(c_ref, w_ref, b_ref, o_ref):
    c = c_ref[...]
    a = (c * (1.0 / (1.0 + jnp.exp(-c)))).astype(BF16)
    o_ref[0] = jnp.dot(a, w_ref[0].astype(BF16), preferred_element_type=F32) + b_ref[0]


def modulation_table(cond, mod_w, mod_b):
    depth, d, n = mod_w.shape
    tn = 1024
    return pl.pallas_call(
        _mod_kernel,
        grid=(depth, n // tn),
        in_specs=[
            pl.BlockSpec((MOD_ROWS, d), lambda l, j: (0, 0)),
            pl.BlockSpec((1, d, tn), lambda l, j: (l, 0, j)),
            pl.BlockSpec((1, 1, tn), lambda l, j: (l, 0, j)),
        ],
        out_specs=pl.BlockSpec((1, MOD_ROWS, tn), lambda l, j: (l, 0, j)),
        out_shape=jax.ShapeDtypeStruct((depth, MOD_ROWS, n), F32),
        name="modulation_table",
        compiler_params=_params(("parallel", "parallel"), VMEM_LIMIT_BIG),
    )(cond, mod_w, mod_b.reshape(depth, 1, n))


def _normmod_kernel(x_ref, nw_ref, shift_ref, scale_ref, o_ref, *, tm, rows_per_mod, mod_base):
    x = x_ref[...]
    y = x * lax.rsqrt(jnp.mean(x * x, axis=-1, keepdims=True) + EPS) * nw_ref[...]
    row = _mod_row(pl.program_id(0), tm, rows_per_mod, mod_base)
    shift = shift_ref[pl.ds(row, 1), :]
    scale = scale_ref[pl.ds(row, 1), :]
    o_ref[...] = (y * (1.0 + scale) + shift).astype(BF16)


def norm_modulate(x, norm_w, mod, piece, *, rows_per_mod, mod_base):
    m, d = x.shape
    tm = 256
    return pl.pallas_call(
        functools.partial(_normmod_kernel, tm=tm, rows_per_mod=rows_per_mod, mod_base=mod_base),
        grid=(m // tm,),
        in_specs=[
            pl.BlockSpec((tm, d), lambda i: (i, 0)),
            pl.BlockSpec((1, d), lambda i: (0, 0)),
            pl.BlockSpec((MOD_ROWS, d), lambda i: (0, piece)),
            pl.BlockSpec((MOD_ROWS, d), lambda i: (0, piece + 1)),
        ],
        out_specs=pl.BlockSpec((tm, d), lambda i: (i, 0)),
        out_shape=jax.ShapeDtypeStruct((m, d), BF16),
        name="norm_modulate",
        compiler_params=_params(("parallel",)),
    )(x, norm_w.reshape(1, d), mod, mod)


def _mm_kernel(*refs, epilogue, tm, rows_per_mod, mod_base):
    if epilogue == "resid":
        x_ref, w_ref, res_ref, gate_ref, o_ref = refs
    else:
        x_ref, w_ref, o_ref = refs
    acc = jnp.dot(x_ref[...], w_ref[...].astype(BF16), preferred_element_type=F32)
    if epilogue == "relu2":
        r = jnp.maximum(acc, 0.0)
        o_ref[...] = (r * r).astype(o_ref.dtype)
    elif epilogue == "resid":
        row = _mod_row(pl.program_id(0), tm, rows_per_mod, mod_base)
        o_ref[...] = res_ref[...] + gate_ref[pl.ds(row, 1), :] * acc
    else:
        o_ref[...] = acc.astype(o_ref.dtype)


def matmul(x, w, *, n_out, tn, out_dtype, col_off=0, epilogue=None, res=None, mod=None, gate_piece=0,
           rows_per_mod=1, mod_base=0, tm=2048):
    m, k = x.shape
    in_specs = [
        pl.BlockSpec((tm, k), lambda i, j: (i, 0), pipeline_mode=pl.Buffered(1)),
        pl.BlockSpec((k, tn), lambda i, j: (0, j + col_off)),
    ]
    args = [x, w]
    if epilogue == "resid":
        gate_off = gate_piece * (D_MODEL // tn)
        in_specs += [
            pl.BlockSpec((tm, tn), lambda i, j: (i, j)),
            pl.BlockSpec((MOD_ROWS, tn), lambda i, j: (0, gate_off + j)),
        ]
        args += [res, mod]
    return pl.pallas_call(
        functools.partial(_mm_kernel, epilogue=epilogue, tm=tm, rows_per_mod=rows_per_mod, mod_base=mod_base),
        grid=(m // tm, n_out // tn),
        in_specs=in_specs,
        out_specs=pl.BlockSpec((tm, tn), lambda i, j: (i, j)),
        out_shape=jax.ShapeDtypeStruct((m, n_out), out_dtype),
        name=f"matmul_{epilogue or 'plain'}",
        compiler_params=_params(("parallel", "parallel"), VMEM_LIMIT_BIG),
    )(*args)


def _mm_ksplit_kernel(x_ref, w_ref, res_ref, gate_ref, o_ref, wb_ref, acc_ref, *, nk, tm, rows_per_mod, mod_base):
    k = pl.program_id(1)
    m = pl.program_id(2)

    @pl.when(m == 0)
    def _():
        wb_ref[...] = w_ref[...].astype(BF16)

    part = jnp.dot(x_ref[...], wb_ref[...], preferred_element_type=F32)
    rows = pl.ds(pl.multiple_of(m * tm, tm), tm)

    @pl.when(k == 0)
    def _():
        acc_ref[rows, :] = part

    @pl.when(jnp.logical_and(k > 0, k < nk - 1))
    def _():
        acc_ref[rows, :] += part

    @pl.when(k == nk - 1)
    def _():
        row = _mod_row(m, tm, rows_per_mod, mod_base)
        o_ref[...] = res_ref[...] + gate_ref[pl.ds(row, 1), :] * (acc_ref[rows, :] + part)


def matmul_ksplit_resid(x, w, res, mod, *, gate_piece, rows_per_mod, mod_base, tm=1024, tn=1024, tk=1024):
    m, kdim = x.shape
    n = w.shape[1]
    nk = kdim // tk
    gate_off = gate_piece * (D_MODEL // tn)
    last = lambda k, mi: jnp.where(k == nk - 1, mi, 0)
    return pl.pallas_call(
        functools.partial(_mm_ksplit_kernel, nk=nk, tm=tm, rows_per_mod=rows_per_mod, mod_base=mod_base),
        grid=(n // tn, nk, m // tm),
        in_specs=[
            pl.BlockSpec((tm, tk), lambda j, k, mi: (mi, k)),
            pl.BlockSpec((tk, tn), lambda j, k, mi: (k, j)),
            pl.BlockSpec((tm, tn), lambda j, k, mi: (last(k, mi), j)),
            pl.BlockSpec((MOD_ROWS, tn), lambda j, k, mi: (0, gate_off + j)),
        ],
        out_specs=pl.BlockSpec((tm, tn), lambda j, k, mi: (last(k, mi), j)),
        out_shape=jax.ShapeDtypeStruct((m, n), F32),
        scratch_shapes=[pltpu.VMEM((tk, tn), BF16), pltpu.VMEM((m, tn), F32)],
        name="matmul_ksplit_resid",
        compiler_params=_params(("arbitrary", "arbitrary", "arbitrary"), VMEM_LIMIT_BIG),
    )(x, w, res, mod)


def _log_sigmoid(x):
    return jnp.minimum(x, 0.0) - jnp.log(1.0 + jnp.exp(-jnp.abs(x)))


def _gate_kernel(h_ref, wg_ref, wgt_ref, b_ref, bt_ref, g_ref, gt_ref):
    h = h_ref[...]
    g = jnp.dot(h, wg_ref[...], preferred_element_type=F32) + b_ref[...]
    gt = lax.dot_general(wgt_ref[...], h, NT_DIMS, preferred_element_type=F32) + bt_ref[...]
    col = lax.broadcasted_iota(jnp.int32, g.shape, 1)
    g_ref[...] = jnp.where((col // MLSTM_HEADS) % 2 == 1, _log_sigmoid(g), g)
    row = lax.broadcasted_iota(jnp.int32, gt.shape, 0)
    gt_ref[...] = jnp.where((row // MLSTM_HEADS) % 2 == 1, _log_sigmoid(gt), gt)


def mlstm_gates(h, w_g, gate_b):
    m, d = h.shape
    ng = MLSTM_N_GATES
    tm = 1024
    wg = w_g.astype(BF16)
    return pl.pallas_call(
        _gate_kernel,
        grid=(m // tm,),
        in_specs=[
            pl.BlockSpec((tm, d), lambda i: (i, 0)),
            pl.BlockSpec((d, ng), lambda i: (0, 0)),
            pl.BlockSpec((ng, d), lambda i: (0, 0)),
            pl.BlockSpec((1, ng), lambda i: (0, 0)),
            pl.BlockSpec((ng, 1), lambda i: (0, 0)),
        ],
        out_specs=[pl.BlockSpec((tm, ng), lambda i: (i, 0)), pl.BlockSpec((ng, tm), lambda i: (0, i))],
        out_shape=[jax.ShapeDtypeStruct((m, ng), F32), jax.ShapeDtypeStruct((ng, m), F32)],
        name="mlstm_gates",
        compiler_params=_params(("parallel",)),
    )(h, wg, wg.T, gate_b.reshape(1, ng), gate_b.reshape(ng, 1))


def _mlstm_direction(q, k, v, i_col, f_col, i_row, f_row, c_ref, n_ref, m_ref, *, causal, has_prev):
    L = q.shape[0]
    t_idx = lax.broadcasted_iota(jnp.int32, (L, L), 0)
    s_idx = lax.broadcasted_iota(jnp.int32, (L, L), 1)
    if causal:
        mask, mask_t = s_idx <= t_idx, t_idx <= s_idx
    else:
        mask, mask_t = s_idx >= t_idx, t_idx >= s_idx
    b_col = jnp.sum(jnp.where(mask, f_row, 0.0), axis=1, keepdims=True)
    b_row = jnp.sum(jnp.where(mask_t, f_col, 0.0), axis=0, keepdims=True)
    d = jnp.where(mask, b_col - b_row + i_row, -jnp.inf)
    m_prev = m_ref[...] if has_prev else jnp.zeros((1, 1), F32)
    m_inter = b_col + m_prev
    m_t = jnp.maximum(m_inter, jnp.max(d, axis=1, keepdims=True))
    s = lax.dot_general(q, k, NT_DIMS, preferred_element_type=F32) * jnp.exp(d - m_t)
    num = jnp.dot(s.astype(BF16), v, preferred_element_type=F32)
    den = jnp.sum(s, axis=1, keepdims=True)
    if has_prev:
        a_inter = jnp.exp(m_inter - m_t)
        num = num + a_inter * jnp.dot(q, c_ref[...].astype(BF16), preferred_element_type=F32)
        den = den + a_inter * jnp.sum(q.astype(F32) * n_ref[...], axis=1, keepdims=True)
    h = num / jnp.maximum(jnp.abs(den), jnp.exp(-m_t))

    b_last = jnp.sum(f_col, axis=0, keepdims=True)
    w_col = b_last - b_col + i_col
    m_new = jnp.maximum(b_last + m_prev, jnp.max(w_col, axis=0, keepdims=True))
    wk = jnp.exp(w_col - m_new) * k.astype(F32)
    c_upd = lax.dot_general(wk.astype(BF16), v, TN_DIMS, preferred_element_type=F32)
    n_upd = jnp.sum(wk, axis=0, keepdims=True)
    if has_prev:
        decay = jnp.exp(b_last + m_prev - m_new)
        c_ref[...] = decay * c_ref[...] + c_upd
        n_ref[...] = decay * n_ref[...] + n_upd
    else:
        c_ref[...] = c_upd
        n_ref[...] = n_upd
    m_ref[...] = m_new
    return h


def _mlstm_kernel(*refs, nc, has_state, emit_state):
    refs = list(refs)
    m0_ref = refs.pop(0) if has_state else None
    qf_ref, kf_ref, vf_ref, gf_ref, gtf_ref = refs[:5]
    refs = refs[5:]
    if nc > 1:
        qb_ref, kb_ref, vb_ref, gb_ref, gtb_ref = refs[:5]
        refs = refs[5:]
    else:
        qb_ref, kb_ref, vb_ref, gb_ref, gtb_ref = qf_ref, kf_ref, vf_ref, gf_ref, gtf_ref
    if has_state:
        c0_ref, n0_ref = refs[:2]
        refs = refs[2:]
    if nc > 1:
        hf_ref, hb_ref = refs[:2]
        refs = refs[2:]
    else:
        hs_ref = refs.pop(0)
    if emit_state:
        cout_ref, nout_ref, mout_ref = refs[:3]
        refs = refs[3:]
    c_sc, n_sc, m_sc = refs

    b = pl.program_id(0)
    head = pl.program_id(1)
    c = pl.program_id(2)
    has_prev = has_state or nc > 1

    if has_prev:
        @pl.when(c == 0)
        def _():
            for dr in range(2):
                if has_state:
                    c_sc[dr] = c0_ref[0, dr, 0]
                    n_sc[dr] = n0_ref[0, dr, 0]
                    m_sc[dr] = jnp.full((1, 1), m0_ref[(b * 2 + dr) * MLSTM_HEADS + head], F32)
                else:
                    c_sc[dr] = jnp.zeros(c_sc.shape[1:], F32)
                    n_sc[dr] = jnp.zeros(n_sc.shape[1:], F32)
                    m_sc[dr] = jnp.zeros((1, 1), F32)

    def gates(g_ref, gt_ref, idx):
        g = g_ref[...]
        lane = lax.broadcasted_iota(jnp.int32, g.shape, 1)
        col = jnp.sum(jnp.where(lane == idx * MLSTM_HEADS + head, g, 0.0), axis=1, keepdims=True)
        return col, gt_ref[pl.ds(idx * MLSTM_HEADS + head, 1), :]

    q_scale = MLSTM_DK ** -0.5
    outs = []
    for dr, (q_ref, k_ref, v_ref, g_ref, gt_ref) in enumerate(
            ((qf_ref, kf_ref, vf_ref, gf_ref, gtf_ref), (qb_ref, kb_ref, vb_ref, gb_ref, gtb_ref))):
        i_col, i_row = gates(g_ref, gt_ref, 2 * dr)
        f_col, f_row = gates(g_ref, gt_ref, 2 * dr + 1)
        outs.append(_mlstm_direction(
            q_ref[...] * q_scale, k_ref[...], v_ref[...], i_col, f_col, i_row, f_row,
            c_sc.at[dr], n_sc.at[dr], m_sc.at[dr], causal=(dr == 0), has_prev=has_prev))
    if nc > 1:
        hf_ref[...] = outs[0]
        hb_ref[...] = outs[1]
    else:
        hs_ref[...] = outs[0] + outs[1]

    if emit_state:
        @pl.when(c == nc - 1)
        def _():
            for dr in range(2):
                cout_ref[0, dr, 0] = c_sc[dr]
                nout_ref[0, dr, 0] = n_sc[dr]
                mout_ref[0, dr, 0] = jnp.broadcast_to(m_sc[dr], (1, LANES))


def mlstm_core(u_qkv, g, gt, *, batch, state=None, emit_state=False):
    m = u_qkv.shape[0]
    L = MLSTM_CHUNK
    nc = m // batch // L
    nh, dk, dv = MLSTM_HEADS, MLSTM_DK, MLSTM_DV
    k_blk0 = MLSTM_QK_W // dk
    v_blk0 = 2 * MLSTM_QK_W // dv
    has_state = state is not None

    def specs(chunk_of):
        rb = lambda b, h, c: b * nc + chunk_of(c)
        return [
            pl.BlockSpec((L, dk), lambda b, h, c: (rb(b, h, c), h)),
            pl.BlockSpec((L, dk), lambda b, h, c: (rb(b, h, c), k_blk0 + h)),
            pl.BlockSpec((L, dv), lambda b, h, c: (rb(b, h, c), v_blk0 + h)),
            pl.BlockSpec((L, MLSTM_N_GATES), lambda b, h, c: (rb(b, h, c), 0)),
            pl.BlockSpec((MLSTM_N_GATES, L), lambda b, h, c: (0, rb(b, h, c))),
        ]

    fwd_chunk = lambda c: c
    bwd_chunk = lambda c: nc - 1 - c
    in_specs, args = [], []
    if has_state:
        c0, n0, m0 = state
        in_specs.append(pl.BlockSpec(memory_space=pltpu.SMEM))
        args.append(m0.reshape(-1))
    in_specs += specs(fwd_chunk)
    args += [u_qkv, u_qkv, u_qkv, g, gt]
    if nc > 1:
        in_specs += specs(bwd_chunk)
        args += [u_qkv, u_qkv, u_qkv, g, gt]
    if has_state:
        in_specs += [
            pl.BlockSpec((1, 2, 1, dk, dv), lambda b, h, c: (b, 0, h, 0, 0)),
            pl.BlockSpec((1, 2, 1, 1, dk), lambda b, h, c: (b, 0, h, 0, 0)),
        ]
        args += [c0, n0.reshape(batch, 2, nh, 1, dk)]

    h_shape = jax.ShapeDtypeStruct((m, MLSTM_V_W), F32)
    if nc > 1:
        out_specs = [
            pl.BlockSpec((L, dv), lambda b, h, c: (b * nc + fwd_chunk(c), h)),
            pl.BlockSpec((L, dv), lambda b, h, c: (b * nc + bwd_chunk(c), h)),
        ]
        out_shape = [h_shape, h_shape]
    else:
        out_specs = [pl.BlockSpec((L, dv), lambda b, h, c: (b, h))]
        out_shape = [h_shape]
    if emit_state:
        out_specs += [
            pl.BlockSpec((1, 2, 1, dk, dv), lambda b, h, c: (b, 0, h, 0, 0)),
            pl.BlockSpec((1, 2, 1, 1, dk), lambda b, h, c: (b, 0, h, 0, 0)),
            pl.BlockSpec((1, 2, 1, 1, LANES), lambda b, h, c: (b, 0, h, 0, 0)),
        ]
        out_shape += [
            jax.ShapeDtypeStruct((batch, 2, nh, dk, dv), F32),
            jax.ShapeDtypeStruct((batch, 2, nh, 1, dk), F32),
            jax.ShapeDtypeStruct((batch, 2, nh, 1, LANES), F32),
        ]
    return pl.pallas_call(
        functools.partial(_mlstm_kernel, nc=nc, has_state=has_state, emit_state=emit_state),
        grid=(batch, nh, nc),
        in_specs=in_specs,
        out_specs=out_specs,
        out_shape=out_shape,
        scratch_shapes=[pltpu.VMEM((2, dk, dv), F32), pltpu.VMEM((2, 1, dk), F32), pltpu.VMEM((2, 1, 1), F32)],
        name="mlstm_core",
        compiler_params=_params(("parallel", "parallel", "arbitrary")),
    )(*args)


def _hnorm_kernel(*refs, n_in):
    h_refs, (o_ref, w_ref, out_ref) = refs[:n_in], refs[n_in:]
    for hd in range(MLSTM_HEADS):
        sl = slice(hd * MLSTM_DV, (hd + 1) * MLSTM_DV)
        x = h_refs[0][:, sl]
        for r in h_refs[1:]:
            x = x + r[:, sl]
        y = x * lax.rsqrt(jnp.mean(x * x, axis=-1, keepdims=True) + EPS) * w_ref[:, sl]
        gate = 1.0 / (1.0 + jnp.exp(-o_ref[:, sl]))
        out_ref[:, sl] = (y * gate).astype(BF16)


def mlstm_hnorm_gate(h_parts, o, hnorm_w):
    m, w = o.shape
    tm = 256
    spec = pl.BlockSpec((tm, w), lambda i: (i, 0))
    return pl.pallas_call(
        functools.partial(_hnorm_kernel, n_in=len(h_parts)),
        grid=(m // tm,),
        in_specs=[spec] * len(h_parts) + [spec, pl.BlockSpec((1, w), lambda i: (0, 0))],
        out_specs=spec,
        out_shape=jax.ShapeDtypeStruct((m, w), BF16),
        name="mlstm_hnorm_gate",
        compiler_params=_params(("parallel",)),
    )(*h_parts, o, hnorm_w.reshape(1, w))


def _rope_tile(y, cos, sin_signed):
    lane = lax.broadcasted_iota(jnp.int32, y.shape, 1)
    partner = jnp.where(lane % ROPE_AXIS_DIM < ROPE_HALF,
                        pltpu.roll(y, LANES - ROPE_HALF, 1), pltpu.roll(y, ROPE_HALF, 1))
    return y * cos + partner * sin_signed


def _mla_lat_kernel(*refs, rope):
    if rope:
        a_ref, qw_ref, kvw_ref, krw_ref, cos_ref, sin_ref, cq_ref, ckv_ref, kr_ref, kr2_ref = refs
    else:
        a_ref, qw_ref, kvw_ref, krw_ref, cq_ref, ckv_ref, kr_ref, kr2_ref = refs
    cq = a_ref[:, :MLA_Q_LORA]
    cq_ref[...] = (cq * lax.rsqrt(jnp.mean(cq * cq, axis=-1, keepdims=True) + EPS) * qw_ref[...]).astype(BF16)
    ckv = a_ref[:, MLA_Q_LORA:MLA_Q_LORA + MLA_KV_LORA]
    ckv_ref[...] = ckv * lax.rsqrt(jnp.mean(ckv * ckv, axis=-1, keepdims=True) + EPS) * kvw_ref[...]
    kr = a_ref[:, MLA_Q_LORA + MLA_KV_LORA:MLA_Q_LORA + MLA_KV_LORA + LANES]
    ms = jnp.sum(kr * kr, axis=-1, keepdims=True) * (1.0 / MLA_ROPE)
    krn = kr * lax.rsqrt(ms + EPS) * krw_ref[...]
    kr_ref[...] = krn
    kr2 = krn + pltpu.roll(krn, MLA_ROPE, 1)
    if rope:
        kr2 = _rope_tile(kr2, cos_ref[...], sin_ref[...])
    kr2_ref[...] = kr2.astype(BF16)


def mla_latents(a, qnorm_w, kvnorm_w, kr_w, rope_tables):
    m = a.shape[0]
    tm = 256
    rope = rope_tables is not None
    in_specs = [
        pl.BlockSpec((tm, MLA_IN_PAD), lambda i: (i, 0)),
        pl.BlockSpec((1, MLA_Q_LORA), lambda i: (0, 0)),
        pl.BlockSpec((1, MLA_KV_LORA), lambda i: (0, 0)),
        pl.BlockSpec((1, LANES), lambda i: (0, 0)),
    ]
    args = [a, qnorm_w.reshape(1, -1), kvnorm_w.reshape(1, -1), kr_w]
    if rope:
        nt = rope_tables[0].shape[0] // tm
        in_specs += [pl.BlockSpec((tm, LANES), lambda i: (i % nt, 0))] * 2
        args += list(rope_tables)
    return pl.pallas_call(
        functools.partial(_mla_lat_kernel, rope=rope),
        grid=(m // tm,),
        in_specs=in_specs,
        out_specs=[
            pl.BlockSpec((tm, MLA_Q_LORA), lambda i: (i, 0)),
            pl.BlockSpec((tm, MLA_KV_LORA), lambda i: (i, 0)),
            pl.BlockSpec((tm, LANES), lambda i: (i, 0)),
            pl.BlockSpec((tm, LANES), lambda i: (i, 0)),
        ],
        out_shape=[
            jax.ShapeDtypeStruct((m, MLA_Q_LORA), BF16),
            jax.ShapeDtypeStruct((m, MLA_KV_LORA), F32),
            jax.ShapeDtypeStruct((m, LANES), F32),
            jax.ShapeDtypeStruct((m, LANES), BF16),
        ],
        name="mla_latents",
        compiler_params=_params(("parallel",)),
    )(*args)


PAIR_W = 2 * MLA_NOPE + LANES
PAIRS_PER_TILE = 2


def _mla_q_kernel(*refs, rope):
    if rope:
        x_ref, w_ref, nw_ref, rw_ref, cos_ref, sin_ref, o_ref = refs
    else:
        x_ref, w_ref, nw_ref, rw_ref, o_ref = refs
    acc = jnp.dot(x_ref[...], w_ref[...].astype(BF16), preferred_element_type=F32)
    lane = lax.broadcasted_iota(jnp.int32, (acc.shape[0], LANES), 1)
    first = lane < MLA_ROPE
    for p in range(PAIRS_PER_TILE):
        base = p * PAIR_W
        obase = p * 2 * MLA_HEAD_PAD
        for hh in range(2):
            x = acc[:, base + hh * MLA_NOPE: base + (hh + 1) * MLA_NOPE]
            y = x * lax.rsqrt(jnp.mean(x * x, axis=-1, keepdims=True) + EPS) * nw_ref[...]
            o_ref[:, obase + hh * MLA_HEAD_PAD: obase + hh * MLA_HEAD_PAD + MLA_NOPE] = y.astype(BF16)
        r = acc[:, base + 2 * MLA_NOPE: base + PAIR_W]
        r2 = r * r
        ss_a = jnp.sum(jnp.where(first, r2, 0.0), axis=-1, keepdims=True)
        ss_b = jnp.sum(jnp.where(first, 0.0, r2), axis=-1, keepdims=True)
        inv = jnp.where(first, lax.rsqrt(ss_a * (1.0 / MLA_ROPE) + EPS), lax.rsqrt(ss_b * (1.0 / MLA_ROPE) + EPS))
        rn = r * inv * rw_ref[...]
        if rope:
            rn = _rope_tile(rn, cos_ref[...], sin_ref[...])
        o_ref[:, obase + MLA_NOPE: obase + MLA_HEAD_PAD] = jnp.where(first, rn, 0.0).astype(BF16)
        o_ref[:, obase + MLA_HEAD_PAD + MLA_NOPE: obase + 2 * MLA_HEAD_PAD] = jnp.where(first, 0.0, rn).astype(BF16)


def mla_queries(cqn, w_uq_pairs, q_nope_w, q_rope_w2, rope_tables):
    m, k = cqn.shape
    tm = 1024
    tn = PAIRS_PER_TILE * PAIR_W
    to = PAIRS_PER_TILE * 2 * MLA_HEAD_PAD
    rope = rope_tables is not None
    in_specs = [
        pl.BlockSpec((tm, k), lambda i, j: (i, 0)),
        pl.BlockSpec((k, tn), lambda i, j: (0, j)),
        pl.BlockSpec((1, MLA_NOPE), lambda i, j: (0, 0)),
        pl.BlockSpec((1, LANES), lambda i, j: (0, 0)),
    ]
    args = [cqn, w_uq_pairs, q_nope_w, q_rope_w2]
    if rope:
        nt = rope_tables[0].shape[0] // tm
        in_specs += [pl.BlockSpec((tm, LANES), lambda i, j: (i % nt, 0))] * 2
        args += list(rope_tables)
    return pl.pallas_call(
        functools.partial(_mla_q_kernel, rope=rope),
        grid=(m // tm, w_uq_pairs.shape[1] // tn),
        in_specs=in_specs,
        out_specs=pl.BlockSpec((tm, to), lambda i, j: (i, j)),
        out_shape=jax.ShapeDtypeStruct((m, MLA_HEADS * MLA_HEAD_PAD), BF16),
        name="mla_queries",
        compiler_params=_params(("parallel", "parallel")),
    )(*args)


KV_HEADS_PER_TILE = 4


def _mla_kv_kernel(x_ref, w_ref, kr2_ref, nw_ref, k_ref, v_ref):
    acc = jnp.dot(x_ref[...], w_ref[...].astype(BF16), preferred_element_type=F32)
    kr2 = kr2_ref[...]
    for hh in range(KV_HEADS_PER_TILE):
        base = hh * (MLA_NOPE + MLA_V)
        x = acc[:, base: base + MLA_NOPE]
        y = x * lax.rsqrt(jnp.mean(x * x, axis=-1, keepdims=True) + EPS) * nw_ref[...]
        k_ref[:, hh * MLA_HEAD_PAD: hh * MLA_HEAD_PAD + MLA_NOPE] = y.astype(BF16)
        k_ref[:, hh * MLA_HEAD_PAD + MLA_NOPE: (hh + 1) * MLA_HEAD_PAD] = kr2
        v_ref[:, hh * MLA_V: (hh + 1) * MLA_V] = acc[:, base + MLA_NOPE: base + MLA_NOPE + MLA_V].astype(BF16)


def mla_keys_values(ckv, kr2, w_ukv, k_nope_w):
    m, k = ckv.shape
    tm = 512
    tn = KV_HEADS_PER_TILE * (MLA_NOPE + MLA_V)
    return pl.pallas_call(
        _mla_kv_kernel,
        grid=(m // tm, w_ukv.shape[1] // tn),
        in_specs=[
            pl.BlockSpec((tm, k), lambda i, j: (i, 0)),
            pl.BlockSpec((k, tn), lambda i, j: (0, j)),
            pl.BlockSpec((tm, LANES), lambda i, j: (i, 0)),
            pl.BlockSpec((1, MLA_NOPE), lambda i, j: (0, 0)),
        ],
        out_specs=[
            pl.BlockSpec((tm, KV_HEADS_PER_TILE * MLA_HEAD_PAD), lambda i, j: (i, j)),
            pl.BlockSpec((tm, KV_HEADS_PER_TILE * MLA_V), lambda i, j: (i, j)),
        ],
        out_shape=[
            jax.ShapeDtypeStruct((m, MLA_HEADS * MLA_HEAD_PAD), BF16),
            jax.ShapeDtypeStruct((m, MLA_HEADS * MLA_V), BF16),
        ],
        name="mla_keys_values",
        compiler_params=_params(("parallel", "parallel")),
    )(ckv, w_ukv, kr2, k_nope_w)


def _attn_kernel(q_ref, k_ref, v_ref, o_ref, *, heads):
    scale = MLA_QK_DIM ** -0.5
    for hh in range(heads):
        q = q_ref[:, hh * MLA_HEAD_PAD:(hh + 1) * MLA_HEAD_PAD]
        k = k_ref[:, hh * MLA_HEAD_PAD:(hh + 1) * MLA_HEAD_PAD]
        s = lax.dot_general(q, k, NT_DIMS, preferred_element_type=F32) * scale
        p = jnp.exp(s - jnp.max(s, axis=-1, keepdims=True))
        o = jnp.dot(p.astype(BF16), v_ref[:, hh * MLA_V:(hh + 1) * MLA_V], preferred_element_type=F32)
        o_ref[:, hh * MLA_V:(hh + 1) * MLA_V] = (o / jnp.sum(p, axis=-1, keepdims=True)).astype(BF16)


def attention(q, k, v, *, batch, heads_per_step, tq):
    m = q.shape[0]
    t = m // batch
    s = k.shape[0] // batch
    nq = t // tq
    hb = heads_per_step
    return pl.pallas_call(
        functools.partial(_attn_kernel, heads=hb),
        grid=(batch, MLA_HEADS // hb, nq),
        in_specs=[
            pl.BlockSpec((tq, hb * MLA_HEAD_PAD), lambda b, g, i: (b * nq + i, g)),
            pl.BlockSpec((s, hb * MLA_HEAD_PAD), lambda b, g, i: (b, g)),
            pl.BlockSpec((s, hb * MLA_V), lambda b, g, i: (b, g)),
        ],
        out_specs=pl.BlockSpec((tq, hb * MLA_V), lambda b, g, i: (b * nq + i, g)),
        out_shape=jax.ShapeDtypeStruct((m, MLA_HEADS * MLA_V), BF16),
        name="attention",
        compiler_params=_params(("parallel", "parallel", "parallel")),
    )(q, k, v)


def _pair_columns():
    cols = []
    for p in range(MLA_HEADS // 2):
        a, b = 2 * p, 2 * p + 1
        cols += [np.arange(a * MLA_QK_DIM, a * MLA_QK_DIM + MLA_NOPE),
                 np.arange(b * MLA_QK_DIM, b * MLA_QK_DIM + MLA_NOPE),
                 np.arange(a * MLA_QK_DIM + MLA_NOPE, (a + 1) * MLA_QK_DIM),
                 np.arange(b * MLA_QK_DIM + MLA_NOPE, (b + 1) * MLA_QK_DIM)]
    return np.concatenate(cols)


def _rope_tables(n_tok):
    pos = np.arange(n_tok)
    row = (pos // GRID_W).astype(np.float32)
    col = (pos % GRID_W).astype(np.float32)
    inv_freq = jnp.asarray(ROPE_BASE, F32) ** (-jnp.arange(0, ROPE_AXIS_DIM, 2, dtype=F32) / ROPE_AXIS_DIM)
    lane = np.arange(LANES)
    freq = inv_freq[lane % ROPE_HALF]
    use_row = (lane % MLA_ROPE) < ROPE_AXIS_DIM
    ang = jnp.where(use_row[None, :], row[:, None], col[:, None]) * freq[None, :]
    sign = np.where(lane % ROPE_AXIS_DIM < ROPE_HALF, -1.0, 1.0).astype(np.float32)
    return jnp.cos(ang), jnp.sin(ang) * sign[None, :]


def _mlstm_layer(h, x, mod, w_in, gate_b, hnorm_w, w_out, *, batch, state, emit_state, mod_kw):
    tn = 512
    u_qkv = matmul(h, w_in, n_out=2 * MLSTM_QK_W + MLSTM_V_W, tn=tn, out_dtype=BF16)
    o = matmul(h, w_in, n_out=MLSTM_V_W, tn=tn, col_off=(2 * MLSTM_QK_W + MLSTM_V_W) // tn, out_dtype=F32)
    g, gt = mlstm_gates(h, w_in[:, 2 * MLSTM_QK_W + 2 * MLSTM_V_W:], gate_b)
    outs = mlstm_core(u_qkv, g, gt, batch=batch, state=state, emit_state=emit_state)
    n_h = len(outs) - (3 if emit_state else 0)
    hg = mlstm_hnorm_gate(outs[:n_h], o, hnorm_w)
    x = matmul(hg, w_out, n_out=D_MODEL, tn=256, out_dtype=F32, epilogue="resid", res=x, mod=mod, gate_piece=2,
               **mod_kw)
    return x, outs[n_h:]


def _mla_layer(h, x, mod, wts, *, batch, ctx, rope_tables, mod_kw):
    w_in_pad, qnorm_w, kvnorm_w, w_uq_pairs, w_ukv, q_nope_w, q_rope_w2, k_nope_w, kr_w, w_out = wts
    a = matmul(h, w_in_pad, n_out=MLA_IN_PAD, tn=512, out_dtype=F32)
    cqn, ckvn, krn, kr2 = mla_latents(a, qnorm_w, kvnorm_w, kr_w, rope_tables)
    q = mla_queries(cqn, w_uq_pairs, q_nope_w, q_rope_w2, rope_tables)
    ckv_all, kr2_all = ckvn.astype(BF16), kr2
    if ctx is not None:
        ckv_ctx, kr_ctx = ctx
        t = ckvn.shape[0] // batch
        kr2_ctx = jnp.concatenate([kr_ctx, kr_ctx], axis=-1).astype(BF16)
        ckv_all = jnp.concatenate([ckv_ctx.astype(BF16), ckv_all.reshape(batch, t, -1)], axis=1).reshape(-1, MLA_KV_LORA)
        kr2_all = jnp.concatenate([kr2_ctx, kr2.reshape(batch, t, -1)], axis=1).reshape(-1, LANES)
    k, v = mla_keys_values(ckv_all, kr2_all, w_ukv, k_nope_w)
    if ctx is None:
        o = attention(q, k, v, batch=batch, heads_per_step=8, tq=q.shape[0] // batch)
    else:
        o = attention(q, k, v, batch=batch, heads_per_step=2, tq=512)
    x = matmul(o, w_out, n_out=D_MODEL, tn=256, out_dtype=F32, epilogue="resid", res=x, mod=mod, gate_piece=2,
               **mod_kw)
    return x, ckvn, krn


def _mlp(x, mod, norm_w, w1, w2, *, mod_kw):
    h = norm_modulate(x, norm_w, mod, 3, **mod_kw)
    u = matmul(h, w1, n_out=D_FF, tn=512, out_dtype=BF16, epilogue="relu2")
    return matmul_ksplit_resid(u, w2, x, mod, gate_piece=5, **mod_kw)


def kernel(x_prompt, x_sample, state_mlstm_C, state_mlstm_n, state_mlstm_m, cache_mla_ckv, cache_mla_krope, c, c_ctx,
           norm1_w, norm2_w, mod_w, mod_b, mlp_w1, mlp_w2,
           mlstm_w_in, mlstm_gate_b, mlstm_hnorm_w, mlstm_w_out,
           mla_w_in, mla_qnorm_w, mla_kvnorm_w, mla_w_uq, mla_w_ukv, mla_q_norm_w, mla_k_norm_w, mla_w_out):
    bp, tp, d = x_prompt.shape
    bs, ts, _ = x_sample.shape
    xp = x_prompt.reshape(bp * tp, d)
    xs = x_sample.reshape(bs * ts, d)
    cond = jnp.zeros((MOD_ROWS, d), F32).at[0].set(c_ctx).at[1:1 + bs].set(c)
    mod = modulation_table(cond, mod_w, mod_b)
    kw_p = dict(rows_per_mod=bp * tp, mod_base=0)
    kw_s = dict(rows_per_mod=ts, mod_base=1)

    wa = (mlstm_w_in[0], mlstm_gate_b[0], mlstm_hnorm_w[0], mlstm_w_out[0])
    hp = norm_modulate(xp, norm1_w[0], mod[0], 0, **kw_p)
    hs = norm_modulate(xs, norm1_w[0], mod[0], 0, **kw_s)
    xp, (c_new, n_new, m_new) = _mlstm_layer(hp, xp, mod[0], *wa, batch=bp, state=None, emit_state=True, mod_kw=kw_p)
    state = (state_mlstm_C[:, 0], state_mlstm_n[:, 0], state_mlstm_m[:, 0])
    xs, _ = _mlstm_layer(hs, xs, mod[0], *wa, batch=bs, state=state, emit_state=False, mod_kw=kw_s)
    xp = _mlp(xp, mod[0], norm2_w[0], mlp_w1[0], mlp_w2[0], mod_kw=kw_p)
    xs = _mlp(xs, mod[0], norm2_w[0], mlp_w1[0], mlp_w2[0], mod_kw=kw_s)

    qw, kw = mla_q_norm_w[0], mla_k_norm_w[0]
    pad = lambda w: jnp.concatenate([w, jnp.zeros_like(w)]).reshape(1, LANES)
    wb = (
        jnp.pad(mla_w_in[0], ((0, 0), (0, MLA_IN_PAD - mla_w_in.shape[2]))),
        mla_qnorm_w[0], mla_kvnorm_w[0],
        jnp.take(mla_w_uq[0], _pair_columns(), axis=1), mla_w_ukv[0],
        qw[:MLA_NOPE].reshape(1, -1), jnp.concatenate([qw[MLA_NOPE:], qw[MLA_NOPE:]]).reshape(1, LANES),
        kw[:MLA_NOPE].reshape(1, -1), pad(kw[MLA_NOPE:]), mla_w_out[0],
    )
    hp = norm_modulate(xp, norm1_w[1], mod[1], 0, **kw_p)
    hs = norm_modulate(xs, norm1_w[1], mod[1], 0, **kw_s)
    xp, ckv_p, kr_p = _mla_layer(hp, xp, mod[1], wb, batch=bp, ctx=None, rope_tables=None, mod_kw=kw_p)
    xs, _, _ = _mla_layer(hs, xs, mod[1], wb, batch=bs, ctx=(cache_mla_ckv[:, 0], cache_mla_krope[:, 0]),
                          rope_tables=_rope_tables(ts), mod_kw=kw_s)
    xp = _mlp(xp, mod[1], norm2_w[1], mlp_w1[1], mlp_w2[1], mod_kw=kw_p)
    xs = _mlp(xs, mod[1], norm2_w[1], mlp_w1[1], mlp_w2[1], mod_kw=kw_s)

    return (
        xp.reshape(bp, tp, d),
        xs.reshape(bs, ts, d),
        c_new.reshape(bp, 1, 2, MLSTM_HEADS, MLSTM_DK, MLSTM_DV),
        n_new.reshape(bp, 1, 2, MLSTM_HEADS, MLSTM_DK),
        m_new[..., 0, 0].reshape(bp, 1, 2, MLSTM_HEADS),
        ckv_p.reshape(bp, 1, tp, MLA_KV_LORA),
        kr_p[:, :MLA_ROPE].reshape(bp, 1, tp, MLA_ROPE),
    )
```

```python
import functools

import jax
import jax.numpy as jnp
import numpy as np
from jax import lax
from jax.experimental import pallas as pl
from jax.experimental.pallas import tpu as pltpu

F32 = jnp.float32
BF16 = jnp.bfloat16

D_MODEL = 4096
D_FF = 4 * D_MODEL
N_MOD = 6
EPS = 1e-6
GRID_W = 64

MLSTM_HEADS = 8
MLSTM_DK = 256
MLSTM_DV = 512
MLSTM_QK_W = MLSTM_HEADS * MLSTM_DK
MLSTM_V_W = MLSTM_HEADS * MLSTM_DV
MLSTM_N_GATES = 4 * MLSTM_HEADS
MLSTM_CHUNK = 256

MLA_HEADS = 32
MLA_NOPE = 128
MLA_ROPE = 64
MLA_V = 128
MLA_QK_DIM = MLA_NOPE + MLA_ROPE
MLA_Q_LORA = 1024
MLA_KV_LORA = 512
MLA_IN_PAD = 2048
MLA_HEAD_PAD = 256
ROPE_AXIS_DIM = MLA_ROPE // 2
ROPE_HALF = ROPE_AXIS_DIM // 2
ROPE_BASE = 10000.0

LANES = 128
SUBLANES = 8
MOD_ROWS = SUBLANES
VMEM_LIMIT_BIG = 58 * 1024 * 1024
VMEM_LIMIT_MED = 40 * 1024 * 1024

NT_DIMS = (((1,), (1,)), ((), ()))
TN_DIMS = (((0,), (0,)), ((), ()))


def _params(semantics, vmem=VMEM_LIMIT_MED):
    return pltpu.CompilerParams(dimension_semantics=semantics, vmem_limit_bytes=vmem)


def _mod_row(tile, tm, rows_per_mod, mod_base):
    return mod_base + (tile * tm) // rows_per_mod


def _mod_kernel(c_ref, w_ref, b_ref, o_ref):
    c = c_ref[...]
    a = (c * (1.0 / (1.0 + jnp.exp(-c)))).astype(BF16)
    o_ref[0] = jnp.dot(a, w_ref[0].astype(BF16), preferred_element_type=F32) + b_ref[0]


def modulation_table(cond, mod_w, mod_b):
    depth, d, n = mod_w.shape
    tn = 1024
    return pl.pallas_call(
        _mod_kernel,
        grid=(depth, n // tn),
        in_specs=[
            pl.BlockSpec((MOD_ROWS, d), lambda l, j: (0, 0)),
            pl.BlockSpec((1, d, tn), lambda l, j: (l, 0, j)),
            pl.BlockSpec((1, 1, tn), lambda l, j: (l, 0, j)),
        ],
        out_specs=pl.BlockSpec((1, MOD_ROWS, tn), lambda l, j: (l, 0, j)),
        out_shape=jax.ShapeDtypeStruct((depth, MOD_ROWS, n), F32),
        name="modulation_table",
        compiler_params=_params(("parallel", "parallel"), VMEM_LIMIT_BIG),
    )(cond, mod_w, mod_b.reshape(depth, 1, n))


def _normmod_kernel(x_ref, nw_ref, shift_ref, scale_ref, o_ref, *, tm, rows_per_mod, mod_base):
    x = x_ref[...]
    y = x * lax.rsqrt(jnp.mean(x * x, axis=-1, keepdims=True) + EPS) * nw_ref[...]
    row = _mod_row(pl.program_id(0), tm, rows_per_mod, mod_base)
    shift = shift_ref[pl.ds(row, 1), :]
    scale = scale_ref[pl.ds(row, 1), :]
    o_ref[...] = (y * (1.0 + scale) + shift).astype(BF16)


def norm_modulate(x, norm_w, mod, piece, *, rows_per_mod, mod_base):
    m, d = x.shape
    tm = 256
    return pl.pallas_call(
        functools.partial(_normmod_kernel, tm=tm, rows_per_mod=rows_per_mod, mod_base=mod_base),
        grid=(m // tm,),
        in_specs=[
            pl.BlockSpec((tm, d), lambda i: (i, 0)),
            pl.BlockSpec((1, d), lambda i: (0, 0)),
            pl.BlockSpec((MOD_ROWS, d), lambda i: (0, piece)),
            pl.BlockSpec((MOD_ROWS, d), lambda i: (0, piece + 1)),
        ],
        out_specs=pl.BlockSpec((tm, d), lambda i: (i, 0)),
        out_shape=jax.ShapeDtypeStruct((m, d), BF16),
        name="norm_modulate",
        compiler_params=_params(("parallel",)),
    )(x, norm_w.reshape(1, d), mod, mod)


def _mm_kernel(*refs, epilogue, tm, rows_per_mod, mod_base):
    if epilogue == "resid":
        x_ref, w_ref, res_ref, gate_ref, o_ref = refs
    else:
        x_ref, w_ref, o_ref = refs
    acc = jnp.dot(x_ref[...], w_ref[...].astype(BF16), preferred_element_type=F32)
    if epilogue == "relu2":
        r = jnp.maximum(acc, 0.0)
        o_ref[...] = (r * r).astype(o_ref.dtype)
    elif epilogue == "resid":
        row = _mod_row(pl.program_id(0), tm, rows_per_mod, mod_base)
        o_ref[...] = res_ref[...] + gate_ref[pl.ds(row, 1), :] * acc
    else:
        o_ref[...] = acc.astype(o_ref.dtype)


def matmul(x, w, *, n_out, tn, out_dtype, layer=0, col_off=0, epilogue=None, res=None, mod=None, gate_piece=0,
           rows_per_mod=1, mod_base=0, tm=2048):
    m, k = x.shape
    in_specs = [
        pl.BlockSpec((tm, k), lambda i, j: (i, 0), pipeline_mode=pl.Buffered(1)),
        pl.BlockSpec((None, k, tn), lambda i, j: (layer, 0, j + col_off)),
    ]
    args = [x, w]
    if epilogue == "resid":
        gate_off = gate_piece * (D_MODEL // tn)
        in_specs += [
            pl.BlockSpec((tm, tn), lambda i, j: (i, j)),
            pl.BlockSpec((MOD_ROWS, tn), lambda i, j: (0, gate_off + j)),
        ]
        args += [res, mod]
    return pl.pallas_call(
        functools.partial(_mm_kernel, epilogue=epilogue, tm=tm, rows_per_mod=rows_per_mod, mod_base=mod_base),
        grid=(m // tm, n_out // tn),
        in_specs=in_specs,
        out_specs=pl.BlockSpec((tm, tn), lambda i, j: (i, j)),
        out_shape=jax.ShapeDtypeStruct((m, n_out), out_dtype),
        name=f"matmul_{epilogue or 'plain'}",
        compiler_params=_params(("parallel", "parallel"), VMEM_LIMIT_BIG),
    )(*args)


def _mm_ksplit_kernel(x_ref, w_ref, res_ref, gate_ref, o_ref, *, nk, tm, rows_per_mod, mod_base):
    k = pl.program_id(2)
    part = jnp.dot(x_ref[...], w_ref[...].astype(BF16), preferred_element_type=F32)

    @pl.when(k == 0)
    def _():
        o_ref[...] = part

    @pl.when(jnp.logical_and(k > 0, k < nk - 1))
    def _():
        o_ref[...] += part

    @pl.when(k == nk - 1)
    def _():
        row = _mod_row(pl.program_id(0), tm, rows_per_mod, mod_base)
        o_ref[...] = res_ref[...] + gate_ref[pl.ds(row, 1), :] * (o_ref[...] + part)


def matmul_ksplit_resid(x, w, res, mod, *, layer, gate_piece, rows_per_mod, mod_base, tm=2048, tn=1024, tk=1024):
    m, kdim = x.shape
    n = w.shape[2]
    nk = kdim // tk
    gate_off = gate_piece * (D_MODEL // tn)
    return pl.pallas_call(
        functools.partial(_mm_ksplit_kernel, nk=nk, tm=tm, rows_per_mod=rows_per_mod, mod_base=mod_base),
        grid=(m // tm, n // tn, nk),
        in_specs=[
            pl.BlockSpec((tm, tk), lambda i, j, k: (i, k)),
            pl.BlockSpec((None, tk, tn), lambda i, j, k: (layer, k, j)),
            pl.BlockSpec((tm, tn), lambda i, j, k: (i, j), pipeline_mode=pl.Buffered(1)),
            pl.BlockSpec((MOD_ROWS, tn), lambda i, j, k: (0, gate_off + j)),
        ],
        out_specs=pl.BlockSpec((tm, tn), lambda i, j, k: (i, j)),
        out_shape=jax.ShapeDtypeStruct((m, n), F32),
        name="matmul_ksplit_resid",
        compiler_params=_params(("parallel", "parallel", "arbitrary"), VMEM_LIMIT_BIG),
    )(x, w, res, mod)


def _log_sigmoid(x):
    return jnp.minimum(x, 0.0) - jnp.log(1.0 + jnp.exp(-jnp.abs(x)))


def _gate_kernel(h_ref, wg_ref, b_ref, g_ref, gt_ref):
    g = jnp.dot(h_ref[...], wg_ref[...].astype(BF16), preferred_element_type=F32)
    col = lax.broadcasted_iota(jnp.int32, g.shape, 1)
    g = jnp.where(col < MLSTM_N_GATES, g, 0.0) + b_ref[...]
    g = jnp.where((col // MLSTM_HEADS) % 2 == 1, _log_sigmoid(g), g)
    g_ref[...] = g[:, :MLSTM_N_GATES]
    gt_ref[...] = g.T[:MLSTM_N_GATES, :]


def mlstm_gates(h, w_in, gate_b, *, layer):
    m, d = h.shape
    ng = MLSTM_N_GATES
    tm = 1024
    gate_blk = (2 * MLSTM_QK_W + 2 * MLSTM_V_W) // LANES
    bias = jnp.zeros((1, LANES), F32).at[0, :ng].set(gate_b)
    return pl.pallas_call(
        _gate_kernel,
        grid=(m // tm,),
        in_specs=[
            pl.BlockSpec((tm, d), lambda i: (i, 0)),
            pl.BlockSpec((None, d, LANES), lambda i: (layer, 0, gate_blk)),
            pl.BlockSpec((1, LANES), lambda i: (0, 0)),
        ],
        out_specs=[pl.BlockSpec((tm, ng), lambda i: (i, 0)), pl.BlockSpec((ng, tm), lambda i: (0, i))],
        out_shape=[jax.ShapeDtypeStruct((m, ng), F32), jax.ShapeDtypeStruct((ng, m), F32)],
        name="mlstm_gates",
        compiler_params=_params(("parallel",)),
    )(h, w_in, bias)


def _mlstm_direction(q, k, v, i_col, f_col, i_row, f_row, c_ref, n_ref, m_ref, *, causal, has_prev):
    L = q.shape[0]
    t_idx = lax.broadcasted_iota(jnp.int32, (L, L), 0)
    s_idx = lax.broadcasted_iota(jnp.int32, (L, L), 1)
    if causal:
        mask, mask_t = s_idx <= t_idx, t_idx <= s_idx
    else:
        mask, mask_t = s_idx >= t_idx, t_idx >= s_idx
    b_col = jnp.sum(jnp.where(mask, f_row, 0.0), axis=1, keepdims=True)
    b_row = jnp.sum(jnp.where(mask_t, f_col, 0.0), axis=0, keepdims=True)
    d = jnp.where(mask, b_col - b_row + i_row, -jnp.inf)
    m_prev = m_ref[...] if has_prev else jnp.zeros((1, 1), F32)
    m_inter = b_col + m_prev
    m_t = jnp.maximum(m_inter, jnp.max(d, axis=1, keepdims=True))
    s = lax.dot_general(q, k, NT_DIMS, preferred_element_type=F32) * jnp.exp(d - m_t)
    num = jnp.dot(s.astype(BF16), v, preferred_element_type=F32)
    den = jnp.sum(s, axis=1, keepdims=True)
    if has_prev:
        a_inter = jnp.exp(m_inter - m_t)
        num = num + a_inter * jnp.dot(q, c_ref[...].astype(BF16), preferred_element_type=F32)
        den = den + a_inter * jnp.sum(q.astype(F32) * n_ref[...], axis=1, keepdims=True)
    h = num / jnp.maximum(jnp.abs(den), jnp.exp(-m_t))

    b_last = jnp.sum(f_col, axis=0, keepdims=True)
    w_col = b_last - b_col + i_col
    m_new = jnp.maximum(b_last + m_prev, jnp.max(w_col, axis=0, keepdims=True))
    wk = jnp.exp(w_col - m_new) * k.astype(F32)
    c_upd = lax.dot_general(wk.astype(BF16), v, TN_DIMS, preferred_element_type=F32)
    n_upd = jnp.sum(wk, axis=0, keepdims=True)
    if has_prev:
        decay = jnp.exp(b_last + m_prev - m_new)
        c_ref[...] = decay * c_ref[...] + c_upd
        n_ref[...] = decay * n_ref[...] + n_upd
    else:
        c_ref[...] = c_upd
        n_ref[...] = n_upd
    m_ref[...] = m_new
    return h


def _mlstm_kernel(*refs, nc, has_state, emit_state):
    refs = list(refs)
    m0_ref = refs.pop(0) if has_state else None
    qf_ref, kf_ref, vf_ref, gf_ref, gtf_ref = refs[:5]
    refs = refs[5:]
    if nc > 1:
        qb_ref, kb_ref, vb_ref, gb_ref, gtb_ref = refs[:5]
        refs = refs[5:]
    else:
        qb_ref, kb_ref, vb_ref, gb_ref, gtb_ref = qf_ref, kf_ref, vf_ref, gf_ref, gtf_ref
    if has_state:
        c0_ref, n0_ref = refs[:2]
        refs = refs[2:]
    if nc > 1:
        hf_ref, hb_ref = refs[:2]
        refs = refs[2:]
    else:
        hs_ref = refs.pop(0)
    if emit_state:
        cout_ref, nout_ref, mout_ref = refs[:3]
        refs = refs[3:]
    c_sc, n_sc, m_sc = refs

    b = pl.program_id(0)
    head = pl.program_id(1)
    c = pl.program_id(2)
    has_prev = has_state or nc > 1

    if has_prev:
        @pl.when(c == 0)
        def _():
            for dr in range(2):
                if has_state:
                    c_sc[dr] = c0_ref[0, dr, 0]
                    n_sc[dr] = n0_ref[0, dr, 0]
                    m_sc[dr] = jnp.full((1, 1), m0_ref[(b * 2 + dr) * MLSTM_HEADS + head], F32)
                else:
                    c_sc[dr] = jnp.zeros(c_sc.shape[1:], F32)
                    n_sc[dr] = jnp.zeros(n_sc.shape[1:], F32)
                    m_sc[dr] = jnp.zeros((1, 1), F32)

    def gates(g_ref, gt_ref, idx):
        g = g_ref[...]
        lane = lax.broadcasted_iota(jnp.int32, g.shape, 1)
        col = jnp.sum(jnp.where(lane == idx * MLSTM_HEADS + head, g, 0.0), axis=1, keepdims=True)
        return col, gt_ref[pl.ds(idx * MLSTM_HEADS + head, 1), :]

    q_scale = MLSTM_DK ** -0.5
    outs = []
    for dr, (q_ref, k_ref, v_ref, g_ref, gt_ref) in enumerate(
            ((qf_ref, kf_ref, vf_ref, gf_ref, gtf_ref), (qb_ref, kb_ref, vb_ref, gb_ref, gtb_ref))):
        i_col, i_row = gates(g_ref, gt_ref, 2 * dr)
        f_col, f_row = gates(g_ref, gt_ref, 2 * dr + 1)
        outs.append(_mlstm_direction(
            q_ref[...] * q_scale, k_ref[...], v_ref[...], i_col, f_col, i_row, f_row,
            c_sc.at[dr], n_sc.at[dr], m_sc.at[dr], causal=(dr == 0), has_prev=has_prev))
    if nc > 1:
        hf_ref[...] = outs[0]
        hb_ref[...] = outs[1]
    else:
        hs_ref[...] = outs[0] + outs[1]

    if emit_state:
        @pl.when(c == nc - 1)
        def _():
            for dr in range(2):
                cout_ref[0, dr, 0] = c_sc[dr]
                nout_ref[0, dr, 0] = n_sc[dr]
                mout_ref[0, dr, 0] = jnp.broadcast_to(m_sc[dr], (1, LANES))


def mlstm_core(u_qkv, g, gt, *, batch, state=None, emit_state=False):
    m = u_qkv.shape[0]
    L = MLSTM_CHUNK
    nc = m // batch // L
    nh, dk, dv = MLSTM_HEADS, MLSTM_DK, MLSTM_DV
    k_blk0 = MLSTM_QK_W // dk
    v_blk0 = 2 * MLSTM_QK_W // dv
    has_state = state is not None

    def specs(chunk_of):
        rb = lambda b, h, c: b * nc + chunk_of(c)
        return [
            pl.BlockSpec((L, dk), lambda b, h, c: (rb(b, h, c), h)),
            pl.BlockSpec((L, dk), lambda b, h, c: (rb(b, h, c), k_blk0 + h)),
            pl.BlockSpec((L, dv), lambda b, h, c: (rb(b, h, c), v_blk0 + h)),
            pl.BlockSpec((L, MLSTM_N_GATES), lambda b, h, c: (rb(b, h, c), 0)),
            pl.BlockSpec((MLSTM_N_GATES, L), lambda b, h, c: (0, rb(b, h, c))),
        ]

    fwd_chunk = lambda c: c
    bwd_chunk = lambda c: nc - 1 - c
    in_specs, args = [], []
    if has_state:
        c0, n0, m0 = state
        in_specs.append(pl.BlockSpec(memory_space=pltpu.SMEM))
        args.append(m0.reshape(-1))
    in_specs += specs(fwd_chunk)
    args += [u_qkv, u_qkv, u_qkv, g, gt]
    if nc > 1:
        in_specs += specs(bwd_chunk)
        args += [u_qkv, u_qkv, u_qkv, g, gt]
    if has_state:
        in_specs += [
            pl.BlockSpec((1, 2, 1, dk, dv), lambda b, h, c: (b, 0, h, 0, 0)),
            pl.BlockSpec((1, 2, 1, 1, dk), lambda b, h, c: (b, 0, h, 0, 0)),
        ]
        args += [c0, n0.reshape(batch, 2, nh, 1, dk)]

    h_shape = jax.ShapeDtypeStruct((m, MLSTM_V_W), F32)
    if nc > 1:
        out_specs = [
            pl.BlockSpec((L, dv), lambda b, h, c: (b * nc + fwd_chunk(c), h)),
            pl.BlockSpec((L, dv), lambda b, h, c: (b * nc + bwd_chunk(c), h)),
        ]
        out_shape = [h_shape, h_shape]
    else:
        out_specs = [pl.BlockSpec((L, dv), lambda b, h, c: (b, h))]
        out_shape = [h_shape]
    if emit_state:
        out_specs += [
            pl.BlockSpec((1, 2, 1, dk, dv), lambda b, h, c: (b, 0, h, 0, 0)),
            pl.BlockSpec((1, 2, 1, 1, dk), lambda b, h, c: (b, 0, h, 0, 0)),
            pl.BlockSpec((1, 2, 1, 1, LANES), lambda b, h, c: (b, 0, h, 0, 0)),
        ]
        out_shape += [
            jax.ShapeDtypeStruct((batch, 2, nh, dk, dv), F32),
            jax.ShapeDtypeStruct((batch, 2, nh, 1, dk), F32),
            jax.ShapeDtypeStruct((batch, 2, nh, 1, LANES), F32),
        ]
    return pl.pallas_call(
        functools.partial(_mlstm_kernel, nc=nc, has_state=has_state, emit_state=emit_state),
        grid=(batch, nh, nc),
        in_specs=in_specs,
        out_specs=out_specs,
        out_shape=out_shape,
        scratch_shapes=[pltpu.VMEM((2, dk, dv), F32), pltpu.VMEM((2, 1, dk), F32), pltpu.VMEM((2, 1, 1), F32)],
        name="mlstm_core",
        compiler_params=_params(("parallel", "parallel", "arbitrary")),
    )(*args)


def _hnorm_kernel(*refs, n_in):
    h_refs, (o_ref, w_ref, out_ref) = refs[:n_in], refs[n_in:]
    for hd in range(MLSTM_HEADS):
        sl = slice(hd * MLSTM_DV, (hd + 1) * MLSTM_DV)
        x = h_refs[0][:, sl]
        for r in h_refs[1:]:
            x = x + r[:, sl]
        y = x * lax.rsqrt(jnp.mean(x * x, axis=-1, keepdims=True) + EPS) * w_ref[:, sl]
        gate = 1.0 / (1.0 + jnp.exp(-o_ref[:, sl]))
        out_ref[:, sl] = (y * gate).astype(BF16)


def mlstm_hnorm_gate(h_parts, o, hnorm_w):
    m, w = o.shape
    tm = 256
    spec = pl.BlockSpec((tm, w), lambda i: (i, 0))
    return pl.pallas_call(
        functools.partial(_hnorm_kernel, n_in=len(h_parts)),
        grid=(m // tm,),
        in_specs=[spec] * len(h_parts) + [spec, pl.BlockSpec((1, w), lambda i: (0, 0))],
        out_specs=spec,
        out_shape=jax.ShapeDtypeStruct((m, w), BF16),
        name="mlstm_hnorm_gate",
        compiler_params=_params(("parallel",)),
    )(*h_parts, o, hnorm_w.reshape(1, w))


def _rope_tile(y, cos, sin_signed):
    lane = lax.broadcasted_iota(jnp.int32, y.shape, 1)
    partner = jnp.where(lane % ROPE_AXIS_DIM < ROPE_HALF,
                        pltpu.roll(y, LANES - ROPE_HALF, 1), pltpu.roll(y, ROPE_HALF, 1))
    return y * cos + partner * sin_signed


def _mla_lat_kernel(*refs, rope):
    if rope:
        a_ref, qw_ref, kvw_ref, krw_ref, cos_ref, sin_ref, cq_ref, ckv_ref, kr_ref, kr2_ref = refs
    else:
        a_ref, qw_ref, kvw_ref, krw_ref, cq_ref, ckv_ref, kr_ref, kr2_ref = refs
    cq = a_ref[:, :MLA_Q_LORA]
    cq_ref[...] = (cq * lax.rsqrt(jnp.mean(cq * cq, axis=-1, keepdims=True) + EPS) * qw_ref[...]).astype(BF16)
    ckv = a_ref[:, MLA_Q_LORA:MLA_Q_LORA + MLA_KV_LORA]
    ckv_ref[...] = ckv * lax.rsqrt(jnp.mean(ckv * ckv, axis=-1, keepdims=True) + EPS) * kvw_ref[...]
    kr = a_ref[:, MLA_Q_LORA + MLA_KV_LORA:MLA_Q_LORA + MLA_KV_LORA + LANES]
    kr = jnp.where(lax.broadcasted_iota(jnp.int32, kr.shape, 1) < MLA_ROPE, kr, 0.0)
    ms = jnp.sum(kr * kr, axis=-1, keepdims=True) * (1.0 / MLA_ROPE)
    krn = kr * lax.rsqrt(ms + EPS) * krw_ref[...]
    kr_ref[...] = krn
    kr2 = krn + pltpu.roll(krn, MLA_ROPE, 1)
    if rope:
        kr2 = _rope_tile(kr2, cos_ref[...], sin_ref[...])
    kr2_ref[...] = kr2.astype(BF16)


def mla_latents(a, qnorm_w, kvnorm_w, kr_w, rope_tables):
    m = a.shape[0]
    tm = 256
    rope = rope_tables is not None
    in_specs = [
        pl.BlockSpec((tm, MLA_IN_PAD), lambda i: (i, 0)),
        pl.BlockSpec((1, MLA_Q_LORA), lambda i: (0, 0)),
        pl.BlockSpec((1, MLA_KV_LORA), lambda i: (0, 0)),
        pl.BlockSpec((1, LANES), lambda i: (0, 0)),
    ]
    args = [a, qnorm_w.reshape(1, -1), kvnorm_w.reshape(1, -1), kr_w]
    if rope:
        nt = rope_tables[0].shape[0] // tm
        in_specs += [pl.BlockSpec((tm, LANES), lambda i: (i % nt, 0))] * 2
        args += list(rope_tables)
    return pl.pallas_call(
        functools.partial(_mla_lat_kernel, rope=rope),
        grid=(m // tm,),
        in_specs=in_specs,
        out_specs=[
            pl.BlockSpec((tm, MLA_Q_LORA), lambda i: (i, 0)),
            pl.BlockSpec((tm, MLA_KV_LORA), lambda i: (i, 0)),
            pl.BlockSpec((tm, LANES), lambda i: (i, 0)),
            pl.BlockSpec((tm, LANES), lambda i: (i, 0)),
        ],
        out_shape=[
            jax.ShapeDtypeStruct((m, MLA_Q_LORA), BF16),
            jax.ShapeDtypeStruct((m, MLA_KV_LORA), F32),
            jax.ShapeDtypeStruct((m, LANES), F32),
            jax.ShapeDtypeStruct((m, LANES), BF16),
        ],
        name="mla_latents",
        compiler_params=_params(("parallel",)),
    )(*args)


PAIR_W = 2 * MLA_QK_DIM
PAIRS_PER_TILE = 2


def _mla_q_kernel(*refs, rope):
    if rope:
        x_ref, w_ref, nw_ref, rw_ref, cos_ref, sin_ref, o_ref = refs
    else:
        x_ref, w_ref, nw_ref, rw_ref, o_ref = refs
    acc = jnp.dot(x_ref[...], w_ref[...].astype(BF16), preferred_element_type=F32)
    lane = lax.broadcasted_iota(jnp.int32, (acc.shape[0], LANES), 1)
    first = lane < MLA_ROPE
    for p in range(PAIRS_PER_TILE):
        base = p * PAIR_W
        obase = p * 2 * MLA_HEAD_PAD
        t1 = acc[:, base + LANES: base + 2 * LANES]
        t2 = acc[:, base + 2 * LANES: base + PAIR_W]
        nopes = (acc[:, base: base + LANES],
                 jnp.where(first, pltpu.roll(t1, MLA_ROPE, 1), pltpu.roll(t2, MLA_ROPE, 1)))
        for hh, x in enumerate(nopes):
            y = x * lax.rsqrt(jnp.mean(x * x, axis=-1, keepdims=True) + EPS) * nw_ref[...]
            o_ref[:, obase + hh * MLA_HEAD_PAD: obase + hh * MLA_HEAD_PAD + MLA_NOPE] = y.astype(BF16)
        r = jnp.where(first, t1, t2)
        r2 = r * r
        ss_a = jnp.sum(jnp.where(first, r2, 0.0), axis=-1, keepdims=True)
        ss_b = jnp.sum(jnp.where(first, 0.0, r2), axis=-1, keepdims=True)
        inv = jnp.where(first, lax.rsqrt(ss_a * (1.0 / MLA_ROPE) + EPS), lax.rsqrt(ss_b * (1.0 / MLA_ROPE) + EPS))
        rn = r * inv * rw_ref[...]
        if rope:
            rn = _rope_tile(rn, cos_ref[...], sin_ref[...])
        o_ref[:, obase + MLA_NOPE: obase + MLA_HEAD_PAD] = jnp.where(first, rn, 0.0).astype(BF16)
        o_ref[:, obase + MLA_HEAD_PAD + MLA_NOPE: obase + 2 * MLA_HEAD_PAD] = jnp.where(first, 0.0, rn).astype(BF16)


def mla_queries(cqn, w_uq, q_nope_w, q_rope_w2, rope_tables, *, layer):
    m, k = cqn.shape
    tm = 1024
    tn = PAIRS_PER_TILE * PAIR_W
    to = PAIRS_PER_TILE * 2 * MLA_HEAD_PAD
    rope = rope_tables is not None
    in_specs = [
        pl.BlockSpec((tm, k), lambda i, j: (i, 0)),
        pl.BlockSpec((None, k, tn), lambda i, j: (layer, 0, j)),
        pl.BlockSpec((1, MLA_NOPE), lambda i, j: (0, 0)),
        pl.BlockSpec((1, LANES), lambda i, j: (0, 0)),
    ]
    args = [cqn, w_uq, q_nope_w, q_rope_w2]
    if rope:
        nt = rope_tables[0].shape[0] // tm
        in_specs += [pl.BlockSpec((tm, LANES), lambda i, j: (i % nt, 0))] * 2
        args += list(rope_tables)
    return pl.pallas_call(
        functools.partial(_mla_q_kernel, rope=rope),
        grid=(m // tm, w_uq.shape[2] // tn),
        in_specs=in_specs,
        out_specs=pl.BlockSpec((tm, to), lambda i, j: (i, j)),
        out_shape=jax.ShapeDtypeStruct((m, MLA_HEADS * MLA_HEAD_PAD), BF16),
        name="mla_queries",
        compiler_params=_params(("parallel", "parallel")),
    )(*args)


KV_HEADS_PER_TILE = 4


def _mla_kv_kernel(x_ref, w_ref, kr2_ref, nw_ref, k_ref, v_ref):
    acc = jnp.dot(x_ref[...], w_ref[...].astype(BF16), preferred_element_type=F32)
    kr2 = kr2_ref[...]
    for hh in range(KV_HEADS_PER_TILE):
        base = hh * (MLA_NOPE + MLA_V)
        x = acc[:, base: base + MLA_NOPE]
        y = x * lax.rsqrt(jnp.mean(x * x, axis=-1, keepdims=True) + EPS) * nw_ref[...]
        k_ref[:, hh * MLA_HEAD_PAD: hh * MLA_HEAD_PAD + MLA_NOPE] = y.astype(BF16)
        k_ref[:, hh * MLA_HEAD_PAD + MLA_NOPE: (hh + 1) * MLA_HEAD_PAD] = kr2
        v_ref[:, hh * MLA_V: (hh + 1) * MLA_V] = acc[:, base + MLA_NOPE: base + MLA_NOPE + MLA_V].astype(BF16)


def mla_keys_values(ckv, kr2, w_ukv, k_nope_w, *, layer):
    m, k = ckv.shape
    tm = 512
    tn = KV_HEADS_PER_TILE * (MLA_NOPE + MLA_V)
    return pl.pallas_call(
        _mla_kv_kernel,
        grid=(m // tm, w_ukv.shape[2] // tn),
        in_specs=[
            pl.BlockSpec((tm, k), lambda i, j: (i, 0)),
            pl.BlockSpec((None, k, tn), lambda i, j: (layer, 0, j)),
            pl.BlockSpec((tm, LANES), lambda i, j: (i, 0)),
            pl.BlockSpec((1, MLA_NOPE), lambda i, j: (0, 0)),
        ],
        out_specs=[
            pl.BlockSpec((tm, KV_HEADS_PER_TILE * MLA_HEAD_PAD), lambda i, j: (i, j)),
            pl.BlockSpec((tm, KV_HEADS_PER_TILE * MLA_V), lambda i, j: (i, j)),
        ],
        out_shape=[
            jax.ShapeDtypeStruct((m, MLA_HEADS * MLA_HEAD_PAD), BF16),
            jax.ShapeDtypeStruct((m, MLA_HEADS * MLA_V), BF16),
        ],
        name="mla_keys_values",
        compiler_params=_params(("parallel", "parallel")),
    )(ckv, w_ukv, kr2, k_nope_w)


def _attn_kernel(q_ref, k_ref, v_ref, o_ref, *, heads):
    scale = MLA_QK_DIM ** -0.5
    for hh in range(heads):
        q = q_ref[:, hh * MLA_HEAD_PAD:(hh + 1) * MLA_HEAD_PAD]
        k = k_ref[:, hh * MLA_HEAD_PAD:(hh + 1) * MLA_HEAD_PAD]
        s = lax.dot_general(q, k, NT_DIMS, preferred_element_type=F32) * scale
        p = jnp.exp(s - jnp.max(s, axis=-1, keepdims=True))
        o = jnp.dot(p.astype(BF16), v_ref[:, hh * MLA_V:(hh + 1) * MLA_V], preferred_element_type=F32)
        o_ref[:, hh * MLA_V:(hh + 1) * MLA_V] = (o / jnp.sum(p, axis=-1, keepdims=True)).astype(BF16)


def attention(q, k, v, *, batch, heads_per_step, tq):
    m = q.shape[0]
    t = m // batch
    s = k.shape[0] // batch
    nq = t // tq
    hb = heads_per_step
    return pl.pallas_call(
        functools.partial(_attn_kernel, heads=hb),
        grid=(batch, MLA_HEADS // hb, nq),
        in_specs=[
            pl.BlockSpec((tq, hb * MLA_HEAD_PAD), lambda b, g, i: (b * nq + i, g)),
            pl.BlockSpec((s, hb * MLA_HEAD_PAD), lambda b, g, i: (b, g)),
            pl.BlockSpec((s, hb * MLA_V), lambda b, g, i: (b, g)),
        ],
        out_specs=pl.BlockSpec((tq, hb * MLA_V), lambda b, g, i: (b * nq + i, g)),
        out_shape=jax.ShapeDtypeStruct((m, MLA_HEADS * MLA_V), BF16),
        name="attention",
        compiler_params=_params(("parallel", "parallel", "parallel")),
    )(q, k, v)


def _rope_tables(n_tok):
    pos = np.arange(n_tok)
    row = (pos // GRID_W).astype(np.float32)
    col = (pos % GRID_W).astype(np.float32)
    inv_freq = jnp.asarray(ROPE_BASE, F32) ** (-jnp.arange(0, ROPE_AXIS_DIM, 2, dtype=F32) / ROPE_AXIS_DIM)
    lane = np.arange(LANES)
    freq = inv_freq[lane % ROPE_HALF]
    use_row = (lane % MLA_ROPE) < ROPE_AXIS_DIM
    ang = jnp.where(use_row[None, :], row[:, None], col[:, None]) * freq[None, :]
    sign = np.where(lane % ROPE_AXIS_DIM < ROPE_HALF, -1.0, 1.0).astype(np.float32)
    return jnp.cos(ang), jnp.sin(ang) * sign[None, :]


def _mlstm_layer(h, x, mod, w_in, gate_b, hnorm_w, w_out, *, layer, batch, state, emit_state, mod_kw):
    tn = 512
    u_qkv = matmul(h, w_in, layer=layer, n_out=2 * MLSTM_QK_W + MLSTM_V_W, tn=tn, out_dtype=BF16)
    o = matmul(h, w_in, layer=layer, n_out=MLSTM_V_W, tn=tn, col_off=(2 * MLSTM_QK_W + MLSTM_V_W) // tn,
               out_dtype=F32)
    g, gt = mlstm_gates(h, w_in, gate_b[layer], layer=layer)
    outs = mlstm_core(u_qkv, g, gt, batch=batch, state=state, emit_state=emit_state)
    n_h = len(outs) - (3 if emit_state else 0)
    hg = mlstm_hnorm_gate(outs[:n_h], o, hnorm_w[layer])
    x = matmul(hg, w_out, layer=layer, n_out=D_MODEL, tn=256, out_dtype=F32, epilogue="resid", res=x, mod=mod,
               gate_piece=2, **mod_kw)
    return x, outs[n_h:]


def _mla_layer(h, x, mod, wts, *, layer, batch, ctx, rope_tables, mod_kw):
    w_in, qnorm_w, kvnorm_w, w_uq, w_ukv, q_nope_w, q_rope_w2, k_nope_w, kr_w, w_out = wts
    a = matmul(h, w_in, layer=layer, n_out=MLA_IN_PAD, tn=512, out_dtype=F32)
    cqn, ckvn, krn, kr2 = mla_latents(a, qnorm_w, kvnorm_w, kr_w, rope_tables)
    q = mla_queries(cqn, w_uq, q_nope_w, q_rope_w2, rope_tables, layer=layer)
    ckv_all, kr2_all = ckvn.astype(BF16), kr2
    if ctx is not None:
        ckv_ctx, kr_ctx = ctx
        t = ckvn.shape[0] // batch
        kr2_ctx = jnp.concatenate([kr_ctx, kr_ctx], axis=-1).astype(BF16)
        ckv_all = jnp.concatenate([ckv_ctx.astype(BF16), ckv_all.reshape(batch, t, -1)], axis=1).reshape(-1, MLA_KV_LORA)
        kr2_all = jnp.concatenate([kr2_ctx, kr2.reshape(batch, t, -1)], axis=1).reshape(-1, LANES)
    k, v = mla_keys_values(ckv_all, kr2_all, w_ukv, k_nope_w, layer=layer)
    if ctx is None:
        o = attention(q, k, v, batch=batch, heads_per_step=8, tq=q.shape[0] // batch)
    else:
        o = attention(q, k, v, batch=batch, heads_per_step=2, tq=512)
    x = matmul(o, w_out, layer=layer, n_out=D_MODEL, tn=256, out_dtype=F32, epilogue="resid", res=x, mod=mod,
               gate_piece=2, **mod_kw)
    return x, ckvn, krn


def _mlp(x, mod, norm_w, w1, w2, *, layer, mod_kw):
    h = norm_modulate(x, norm_w[layer], mod, 3, **mod_kw)
    u = matmul(h, w1, layer=layer, n_out=D_FF, tn=512, out_dtype=BF16, epilogue="relu2")
    return matmul_ksplit_resid(u, w2, x, mod, layer=layer, gate_piece=5, **mod_kw)


def kernel(x_prompt, x_sample, state_mlstm_C, state_mlstm_n, state_mlstm_m, cache_mla_ckv, cache_mla_krope, c, c_ctx,
           norm1_w, norm2_w, mod_w, mod_b, mlp_w1, mlp_w2,
           mlstm_w_in, mlstm_gate_b, mlstm_hnorm_w, mlstm_w_out,
           mla_w_in, mla_qnorm_w, mla_kvnorm_w, mla_w_uq, mla_w_ukv, mla_q_norm_w, mla_k_norm_w, mla_w_out):
    bp, tp, d = x_prompt.shape
    bs, ts, _ = x_sample.shape
    xp = x_prompt.reshape(bp * tp, d)
    xs = x_sample.reshape(bs * ts, d)
    cond = jnp.zeros((MOD_ROWS, d), F32).at[0].set(c_ctx).at[1:1 + bs].set(c)
    mod = modulation_table(cond, mod_w, mod_b)
    kw_p = dict(rows_per_mod=bp * tp, mod_base=0)
    kw_s = dict(rows_per_mod=ts, mod_base=1)

    wa = (mlstm_w_in, mlstm_gate_b, mlstm_hnorm_w, mlstm_w_out)
    hp = norm_modulate(xp, norm1_w[0], mod[0], 0, **kw_p)
    hs = norm_modulate(xs, norm1_w[0], mod[0], 0, **kw_s)
    xp, (c_new, n_new, m_new) = _mlstm_layer(hp, xp, mod[0], *wa, layer=0, batch=bp, state=None, emit_state=True,
                                             mod_kw=kw_p)
    state = (state_mlstm_C[:, 0], state_mlstm_n[:, 0], state_mlstm_m[:, 0])
    xs, _ = _mlstm_layer(hs, xs, mod[0], *wa, layer=0, batch=bs, state=state, emit_state=False, mod_kw=kw_s)
    xp = _mlp(xp, mod[0], norm2_w, mlp_w1, mlp_w2, layer=0, mod_kw=kw_p)
    xs = _mlp(xs, mod[0], norm2_w, mlp_w1, mlp_w2, layer=0, mod_kw=kw_s)

    qw, kw = mla_q_norm_w[0], mla_k_norm_w[0]
    pad = lambda w: jnp.concatenate([w, jnp.zeros_like(w)]).reshape(1, LANES)
    wb = (
        mla_w_in, mla_qnorm_w[0], mla_kvnorm_w[0], mla_w_uq, mla_w_ukv,
        qw[:MLA_NOPE].reshape(1, -1), jnp.concatenate([qw[MLA_NOPE:], qw[MLA_NOPE:]]).reshape(1, LANES),
        kw[:MLA_NOPE].reshape(1, -1), pad(kw[MLA_NOPE:]), mla_w_out,
    )
    hp = norm_modulate(xp, norm1_w[1], mod[1], 0, **kw_p)
    hs = norm_modulate(xs, norm1_w[1], mod[1], 0, **kw_s)
    xp, ckv_p, kr_p = _mla_layer(hp, xp, mod[1], wb, layer=0, batch=bp, ctx=None, rope_tables=None, mod_kw=kw_p)
    xs, _, _ = _mla_layer(hs, xs, mod[1], wb, layer=0, batch=bs, ctx=(cache_mla_ckv[:, 0], cache_mla_krope[:, 0]),
                          rope_tables=_rope_tables(ts), mod_kw=kw_s)
    xp = _mlp(xp, mod[1], norm2_w, mlp_w1, mlp_w2, layer=1, mod_kw=kw_p)
    xs = _mlp(xs, mod[1], norm2_w, mlp_w1, mlp_w2, layer=1, mod_kw=kw_s)

    return (
        xp.reshape(bp, tp, d),
        xs.reshape(bs, ts, d),
        c_new.reshape(bp, 1, 2, MLSTM_HEADS, MLSTM_DK, MLSTM_DV),
        n_new.reshape(bp, 1, 2, MLSTM_HEADS, MLSTM_DK),
        m_new[..., 0, 0].reshape(bp, 1, 2, MLSTM_HEADS),
        ckv_p.reshape(bp, 1, tp, MLA_KV_LORA),
        kr_p[:, :MLA_ROPE].reshape(bp, 1, tp, MLA_ROPE),
    )
```

```python
import functools

import jax
import jax.numpy as jnp
import numpy as np
from jax import lax
from jax.experimental import pallas as pl
from jax.experimental.pallas import tpu as pltpu

F32 = jnp.float32
BF16 = jnp.bfloat16

D_MODEL = 4096
D_FF = 4 * D_MODEL
N_MOD = 6
EPS = 1e-6
GRID_W = 64

MLSTM_HEADS = 8
MLSTM_DK = 256
MLSTM_DV = 512
MLSTM_QK_W = MLSTM_HEADS * MLSTM_DK
MLSTM_V_W = MLSTM_HEADS * MLSTM_DV
MLSTM_N_GATES = 4 * MLSTM_HEADS
MLSTM_CHUNK = 256

MLA_HEADS = 32
MLA_NOPE = 128
MLA_ROPE = 64
MLA_V = 128
MLA_QK_DIM = MLA_NOPE + MLA_ROPE
MLA_Q_LORA = 1024
MLA_KV_LORA = 512
MLA_IN_PAD = 2048
MLA_HEAD_PAD = 256
ROPE_AXIS_DIM = MLA_ROPE // 2
ROPE_HALF = ROPE_AXIS_DIM // 2
ROPE_BASE = 10000.0

LANES = 128
SUBLANES = 8
MOD_ROWS = SUBLANES
VMEM_LIMIT_BIG = 58 * 1024 * 1024
VMEM_LIMIT_MED = 40 * 1024 * 1024

NT_DIMS = (((1,), (1,)), ((), ()))
TN_DIMS = (((0,), (0,)), ((), ()))


def _params(semantics, vmem=VMEM_LIMIT_MED):
    return pltpu.CompilerParams(dimension_semantics=semantics, vmem_limit_bytes=vmem)


def _mod_row(tile, tm, rows_per_mod, mod_base):
    return mod_base + (tile * tm) // rows_per_mod


def _mod_kernel(c_ref, w_ref, b_ref, o_ref):
    c = c_ref[...]
    a = (c * (1.0 / (1.0 + jnp.exp(-c)))).astype(BF16)
    o_ref[0] = jnp.dot(a, w_ref[0].astype(BF16), preferred_element_type=F32) + b_ref[0]


def modulation_table(cond, mod_w, mod_b):
    depth, d, n = mod_w.shape
    tn = 1024
    return pl.pallas_call(
        _mod_kernel,
        grid=(depth, n // tn),
        in_specs=[
            pl.BlockSpec((MOD_ROWS, d), lambda l, j: (0, 0)),
            pl.BlockSpec((1, d, tn), lambda l, j: (l, 0, j)),
            pl.BlockSpec((1, 1, tn), lambda l, j: (l, 0, j)),
        ],
        out_specs=pl.BlockSpec((1, MOD_ROWS, tn), lambda l, j: (l, 0, j)),
        out_shape=jax.ShapeDtypeStruct((depth, MOD_ROWS, n), F32),
        name="modulation_table",
        compiler_params=_params(("parallel", "parallel"), VMEM_LIMIT_BIG),
    )(cond, mod_w, mod_b.reshape(depth, 1, n))


def _normmod_kernel(x_ref, nw_ref, shift_ref, scale_ref, o_ref, *, tm, rows_per_mod, mod_base):
    x = x_ref[...]
    y = x * lax.rsqrt(jnp.mean(x * x, axis=-1, keepdims=True) + EPS) * nw_ref[...]
    row = _mod_row(pl.program_id(0), tm, rows_per_mod, mod_base)
    shift = shift_ref[pl.ds(row, 1), :]
    scale = scale_ref[pl.ds(row, 1), :]
    o_ref[...] = (y * (1.0 + scale) + shift).astype(BF16)


def norm_modulate(x, norm_w, mod, piece, *, rows_per_mod, mod_base):
    m, d = x.shape
    tm = 256
    return pl.pallas_call(
        functools.partial(_normmod_kernel, tm=tm, rows_per_mod=rows_per_mod, mod_base=mod_base),
        grid=(m // tm,),
        in_specs=[
            pl.BlockSpec((tm, d), lambda i: (i, 0)),
            pl.BlockSpec((1, d), lambda i: (0, 0)),
            pl.BlockSpec((MOD_ROWS, d), lambda i: (0, piece)),
            pl.BlockSpec((MOD_ROWS, d), lambda i: (0, piece + 1)),
        ],
        out_specs=pl.BlockSpec((tm, d), lambda i: (i, 0)),
        out_shape=jax.ShapeDtypeStruct((m, d), BF16),
        name="norm_modulate",
        compiler_params=_params(("parallel",)),
    )(x, norm_w.reshape(1, d), mod, mod)


def _mm_kernel(*refs, epilogue, w_is_nk, tm, rows_per_mod, mod_base):
    if epilogue == "resid":
        x_ref, w_ref, res_ref, gate_ref, o_ref = refs
    else:
        x_ref, w_ref, o_ref = refs
    if w_is_nk:
        acc = lax.dot_general(x_ref[...], w_ref[...].astype(BF16), NT_DIMS, preferred_element_type=F32)
    else:
        acc = jnp.dot(x_ref[...], w_ref[...].astype(BF16), preferred_element_type=F32)
    if epilogue == "relu2":
        r = jnp.maximum(acc, 0.0)
        o_ref[...] = (r * r).astype(o_ref.dtype)
    elif epilogue == "resid":
        row = _mod_row(pl.program_id(0), tm, rows_per_mod, mod_base)
        o_ref[...] = res_ref[...] + gate_ref[pl.ds(row, 1), :] * acc
    else:
        o_ref[...] = acc.astype(o_ref.dtype)


def matmul(x, w, *, n_out, tn, out_dtype, layer=0, col_off=0, w_is_nk=False, epilogue=None, res=None, mod=None,
           gate_piece=0, rows_per_mod=1, mod_base=0, tm=2048):
    m, k = x.shape
    if w_is_nk:
        w_spec = pl.BlockSpec((None, tn, k), lambda i, j: (layer, j + col_off, 0))
    else:
        w_spec = pl.BlockSpec((None, k, tn), lambda i, j: (layer, 0, j + col_off))
    in_specs = [pl.BlockSpec((tm, k), lambda i, j: (i, 0), pipeline_mode=pl.Buffered(1)), w_spec]
    args = [x, w]
    if epilogue == "resid":
        gate_off = gate_piece * (D_MODEL // tn)
        in_specs += [
            pl.BlockSpec((tm, tn), lambda i, j: (i, j)),
            pl.BlockSpec((MOD_ROWS, tn), lambda i, j: (0, gate_off + j)),
        ]
        args += [res, mod]
    return pl.pallas_call(
        functools.partial(_mm_kernel, epilogue=epilogue, w_is_nk=w_is_nk, tm=tm, rows_per_mod=rows_per_mod,
                          mod_base=mod_base),
        grid=(m // tm, n_out // tn),
        in_specs=in_specs,
        out_specs=pl.BlockSpec((tm, tn), lambda i, j: (i, j)),
        out_shape=jax.ShapeDtypeStruct((m, n_out), out_dtype),
        name=f"matmul_{epilogue or 'plain'}",
        compiler_params=_params(("parallel", "parallel"), VMEM_LIMIT_BIG),
    )(*args)


def _mm_ksplit_kernel(x_ref, w_ref, res_ref, gate_ref, o_ref, *, nk, tm, rows_per_mod, mod_base):
    k = pl.program_id(2)

    def part():
        return jnp.dot(x_ref[...], w_ref[...].astype(BF16), preferred_element_type=F32)

    @pl.when(k == 0)
    def _():
        o_ref[...] = part()

    @pl.when(jnp.logical_and(k > 0, k < nk - 1))
    def _():
        o_ref[...] = o_ref[...] + part()

    @pl.when(k == nk - 1)
    def _():
        row = _mod_row(pl.program_id(0), tm, rows_per_mod, mod_base)
        o_ref[...] = res_ref[...] + gate_ref[pl.ds(row, 1), :] * (o_ref[...] + part())


def matmul_ksplit_resid(x, w, res, mod, *, layer, gate_piece, rows_per_mod, mod_base, tm=2048, tn=1024, tk=1024):
    m, kdim = x.shape
    n = w.shape[2]
    nk = kdim // tk
    gate_off = gate_piece * (D_MODEL // tn)
    return pl.pallas_call(
        functools.partial(_mm_ksplit_kernel, nk=nk, tm=tm, rows_per_mod=rows_per_mod, mod_base=mod_base),
        grid=(m // tm, n // tn, nk),
        in_specs=[
            pl.BlockSpec((tm, tk), lambda i, j, k: (i, k)),
            pl.BlockSpec((None, tk, tn), lambda i, j, k: (layer, k, j)),
            pl.BlockSpec((tm, tn), lambda i, j, k: (i, j), pipeline_mode=pl.Buffered(1)),
            pl.BlockSpec((MOD_ROWS, tn), lambda i, j, k: (0, gate_off + j)),
        ],
        out_specs=pl.BlockSpec((tm, tn), lambda i, j, k: (i, j)),
        out_shape=jax.ShapeDtypeStruct((m, n), F32),
        name="matmul_ksplit_resid",
        compiler_params=_params(("parallel", "parallel", "arbitrary"), VMEM_LIMIT_BIG),
    )(x, w, res, mod)


def _log_sigmoid(x):
    return jnp.minimum(x, 0.0) - jnp.log(1.0 + jnp.exp(-jnp.abs(x)))


def _gate_kernel(h_ref, wg_ref, b_ref, g_ref, gt_ref):
    gt = lax.dot_general(wg_ref[...].astype(BF16), h_ref[...], NT_DIMS, preferred_element_type=F32)
    row = lax.broadcasted_iota(jnp.int32, gt.shape, 0)
    gt = jnp.where(row < MLSTM_N_GATES, gt, 0.0) + b_ref[...]
    gt = jnp.where((row // MLSTM_HEADS) % 2 == 1, _log_sigmoid(gt), gt)
    gt_ref[...] = gt[:MLSTM_N_GATES, :]
    g_ref[...] = gt.T[:, :MLSTM_N_GATES]


def mlstm_gates(h, w_in_t, gate_b, *, layer):
    m, d = h.shape
    ng = MLSTM_N_GATES
    tm = 1024
    gate_blk = (2 * MLSTM_QK_W + 2 * MLSTM_V_W) // LANES
    bias = jnp.zeros((LANES, 1), F32).at[:ng, 0].set(gate_b)
    return pl.pallas_call(
        _gate_kernel,
        grid=(m // tm,),
        in_specs=[
            pl.BlockSpec((tm, d), lambda i: (i, 0)),
            pl.BlockSpec((None, LANES, d), lambda i: (layer, gate_blk, 0)),
            pl.BlockSpec((LANES, 1), lambda i: (0, 0)),
        ],
        out_specs=[pl.BlockSpec((tm, ng), lambda i: (i, 0)), pl.BlockSpec((ng, tm), lambda i: (0, i))],
        out_shape=[jax.ShapeDtypeStruct((m, ng), F32), jax.ShapeDtypeStruct((ng, m), F32)],
        name="mlstm_gates",
        compiler_params=_params(("parallel",)),
    )(h, w_in_t, bias)


def _mlstm_direction(q, k, v, i_col, f_col, i_row, f_row, c_ref, n_ref, m_ref, *, causal, has_prev):
    L = q.shape[0]
    t_idx = lax.broadcasted_iota(jnp.int32, (L, L), 0)
    s_idx = lax.broadcasted_iota(jnp.int32, (L, L), 1)
    if causal:
        mask, mask_t = s_idx <= t_idx, t_idx <= s_idx
    else:
        mask, mask_t = s_idx >= t_idx, t_idx >= s_idx
    b_col = jnp.sum(jnp.where(mask, f_row, 0.0), axis=1, keepdims=True)
    b_row = jnp.sum(jnp.where(mask_t, f_col, 0.0), axis=0, keepdims=True)
    d = jnp.where(mask, b_col - b_row + i_row, -jnp.inf)
    m_prev = m_ref[...] if has_prev else jnp.zeros((1, 1), F32)
    m_inter = b_col + m_prev
    m_t = jnp.maximum(m_inter, jnp.max(d, axis=1, keepdims=True))
    s = lax.dot_general(q, k, NT_DIMS, preferred_element_type=F32) * jnp.exp(d - m_t)
    num = jnp.dot(s.astype(BF16), v, preferred_element_type=F32)
    den = jnp.sum(s, axis=1, keepdims=True)
    if has_prev:
        a_inter = jnp.exp(m_inter - m_t)
        num = num + a_inter * jnp.dot(q, c_ref[...].astype(BF16), preferred_element_type=F32)
        den = den + a_inter * jnp.sum(q.astype(F32) * n_ref[...], axis=1, keepdims=True)
    h = num / jnp.maximum(jnp.abs(den), jnp.exp(-m_t))

    b_last = jnp.sum(f_col, axis=0, keepdims=True)
    w_col = b_last - b_col + i_col
    m_new = jnp.maximum(b_last + m_prev, jnp.max(w_col, axis=0, keepdims=True))
    wk = jnp.exp(w_col - m_new) * k.astype(F32)
    c_upd = lax.dot_general(wk.astype(BF16), v, TN_DIMS, preferred_element_type=F32)
    n_upd = jnp.sum(wk, axis=0, keepdims=True)
    if has_prev:
        decay = jnp.exp(b_last + m_prev - m_new)
        c_ref[...] = decay * c_ref[...] + c_upd
        n_ref[...] = decay * n_ref[...] + n_upd
    else:
        c_ref[...] = c_upd
        n_ref[...] = n_upd
    m_ref[...] = m_new
    return h


def _mlstm_kernel(*refs, nc, hps, has_state, emit_state):
    refs = list(refs)
    m0_ref = refs.pop(0) if has_state else None
    qf_ref, kf_ref, vf_ref, gf_ref, gtf_ref = refs[:5]
    refs = refs[5:]
    if nc > 1:
        qb_ref, kb_ref, vb_ref, gb_ref, gtb_ref = refs[:5]
        refs = refs[5:]
    else:
        qb_ref, kb_ref, vb_ref, gb_ref, gtb_ref = qf_ref, kf_ref, vf_ref, gf_ref, gtf_ref
    if has_state:
        c0_ref, n0_ref = refs[:2]
        refs = refs[2:]
    if nc > 1:
        hf_ref, hb_ref = refs[:2]
        refs = refs[2:]
    else:
        hs_ref = refs.pop(0)
    if emit_state:
        cout_ref, nout_ref, mout_ref = refs[:3]
        refs = refs[3:]
    c_sc, n_sc, m_sc = refs

    b = pl.program_id(0)
    head0 = pl.program_id(1) * hps
    c = pl.program_id(2)
    has_prev = has_state or nc > 1
    dk, dv = MLSTM_DK, MLSTM_DV

    if has_prev:
        @pl.when(c == 0)
        def _():
            for dr in range(2):
                for j in range(hps):
                    if has_state:
                        c_sc[dr, j] = c0_ref[0, dr, j]
                        n_sc[dr, j] = n0_ref[0, dr, j]
                        m_sc[dr, j] = jnp.full((1, 1), m0_ref[(b * 2 + dr) * MLSTM_HEADS + head0 + j], F32)
                    else:
                        c_sc[dr, j] = jnp.zeros((dk, dv), F32)
                        n_sc[dr, j] = jnp.zeros((1, dk), F32)
                        m_sc[dr, j] = jnp.zeros((1, 1), F32)

    def gates(g_ref, gt_ref, idx, j):
        g = g_ref[...]
        lane = lax.broadcasted_iota(jnp.int32, g.shape, 1)
        sel = idx * MLSTM_HEADS + head0 + j
        col = jnp.sum(jnp.where(lane == sel, g, 0.0), axis=1, keepdims=True)
        return col, gt_ref[pl.ds(sel, 1), :]

    q_scale = MLSTM_DK ** -0.5
    for j in range(hps):
        outs = []
        for dr, (q_ref, k_ref, v_ref, g_ref, gt_ref) in enumerate(
                ((qf_ref, kf_ref, vf_ref, gf_ref, gtf_ref), (qb_ref, kb_ref, vb_ref, gb_ref, gtb_ref))):
            i_col, i_row = gates(g_ref, gt_ref, 2 * dr, j)
            f_col, f_row = gates(g_ref, gt_ref, 2 * dr + 1, j)
            outs.append(_mlstm_direction(
                q_ref[:, j * dk:(j + 1) * dk] * q_scale, k_ref[:, j * dk:(j + 1) * dk], v_ref[:, j * dv:(j + 1) * dv],
                i_col, f_col, i_row, f_row, c_sc.at[dr, j], n_sc.at[dr, j], m_sc.at[dr, j],
                causal=(dr == 0), has_prev=has_prev))
        if nc > 1:
            hf_ref[:, j * dv:(j + 1) * dv] = outs[0]
            hb_ref[:, j * dv:(j + 1) * dv] = outs[1]
        else:
            hs_ref[:, j * dv:(j + 1) * dv] = outs[0] + outs[1]

    if emit_state:
        @pl.when(c == nc - 1)
        def _():
            for dr in range(2):
                for j in range(hps):
                    cout_ref[0, dr, j] = c_sc[dr, j]
                    nout_ref[0, dr, j] = n_sc[dr, j]
                    mout_ref[0, dr, j] = jnp.broadcast_to(m_sc[dr, j], (1, LANES))


def mlstm_core(u_qkv, g, gt, *, batch, state=None, emit_state=False, heads_per_step=4):
    m = u_qkv.shape[0]
    L = MLSTM_CHUNK
    nc = m // batch // L
    hps = heads_per_step
    nh, dk, dv = MLSTM_HEADS, hps * MLSTM_DK, hps * MLSTM_DV
    k_blk0 = MLSTM_QK_W // dk
    v_blk0 = 2 * MLSTM_QK_W // dv
    has_state = state is not None

    def specs(chunk_of):
        rb = lambda b, h, c: b * nc + chunk_of(c)
        return [
            pl.BlockSpec((L, dk), lambda b, h, c: (rb(b, h, c), h)),
            pl.BlockSpec((L, dk), lambda b, h, c: (rb(b, h, c), k_blk0 + h)),
            pl.BlockSpec((L, dv), lambda b, h, c: (rb(b, h, c), v_blk0 + h)),
            pl.BlockSpec((L, MLSTM_N_GATES), lambda b, h, c: (rb(b, h, c), 0)),
            pl.BlockSpec((MLSTM_N_GATES, L), lambda b, h, c: (0, rb(b, h, c))),
        ]

    fwd_chunk = lambda c: c
    bwd_chunk = lambda c: nc - 1 - c
    in_specs, args = [], []
    if has_state:
        c0, n0, m0 = state
        in_specs.append(pl.BlockSpec(memory_space=pltpu.SMEM))
        args.append(m0.reshape(-1))
    in_specs += specs(fwd_chunk)
    args += [u_qkv, u_qkv, u_qkv, g, gt]
    if nc > 1:
        in_specs += specs(bwd_chunk)
        args += [u_qkv, u_qkv, u_qkv, g, gt]
    state_specs = [
        pl.BlockSpec((1, 2, hps, MLSTM_DK, MLSTM_DV), lambda b, h, c: (b, 0, h, 0, 0)),
        pl.BlockSpec((1, 2, hps, 1, MLSTM_DK), lambda b, h, c: (b, 0, h, 0, 0)),
    ]
    if has_state:
        in_specs += state_specs
        args += [c0, n0.reshape(batch, 2, nh, 1, MLSTM_DK)]

    h_shape = jax.ShapeDtypeStruct((m, MLSTM_V_W), F32)
    if nc > 1:
        out_specs = [
            pl.BlockSpec((L, dv), lambda b, h, c: (b * nc + fwd_chunk(c), h)),
            pl.BlockSpec((L, dv), lambda b, h, c: (b * nc + bwd_chunk(c), h)),
        ]
        out_shape = [h_shape, h_shape]
    else:
        out_specs = [pl.BlockSpec((L, dv), lambda b, h, c: (b, h))]
        out_shape = [h_shape]
    if emit_state:
        out_specs += state_specs + [pl.BlockSpec((1, 2, hps, 1, LANES), lambda b, h, c: (b, 0, h, 0, 0))]
        out_shape += [
            jax.ShapeDtypeStruct((batch, 2, nh, MLSTM_DK, MLSTM_DV), F32),
            jax.ShapeDtypeStruct((batch, 2, nh, 1, MLSTM_DK), F32),
            jax.ShapeDtypeStruct((batch, 2, nh, 1, LANES), F32),
        ]
    return pl.pallas_call(
        functools.partial(_mlstm_kernel, nc=nc, hps=hps, has_state=has_state, emit_state=emit_state),
        grid=(batch, nh // hps, nc),
        in_specs=in_specs,
        out_specs=out_specs,
        out_shape=out_shape,
        scratch_shapes=[pltpu.VMEM((2, hps, MLSTM_DK, MLSTM_DV), F32), pltpu.VMEM((2, hps, 1, MLSTM_DK), F32),
                        pltpu.VMEM((2, hps, 1, 1), F32)],
        name="mlstm_core",
        compiler_params=_params(("parallel", "parallel", "arbitrary")),
    )(*args)


def _hnorm_kernel(*refs, n_in):
    h_refs, (o_ref, w_ref, out_ref) = refs[:n_in], refs[n_in:]
    for hd in range(MLSTM_HEADS):
        sl = slice(hd * MLSTM_DV, (hd + 1) * MLSTM_DV)
        x = h_refs[0][:, sl]
        for r in h_refs[1:]:
            x = x + r[:, sl]
        y = x * lax.rsqrt(jnp.mean(x * x, axis=-1, keepdims=True) + EPS) * w_ref[:, sl]
        gate = 1.0 / (1.0 + jnp.exp(-o_ref[:, sl]))
        out_ref[:, sl] = (y * gate).astype(BF16)


def mlstm_hnorm_gate(h_parts, o, hnorm_w):
    m, w = o.shape
    tm = 256
    spec = pl.BlockSpec((tm, w), lambda i: (i, 0))
    return pl.pallas_call(
        functools.partial(_hnorm_kernel, n_in=len(h_parts)),
        grid=(m // tm,),
        in_specs=[spec] * len(h_parts) + [spec, pl.BlockSpec((1, w), lambda i: (0, 0))],
        out_specs=spec,
        out_shape=jax.ShapeDtypeStruct((m, w), BF16),
        name="mlstm_hnorm_gate",
        compiler_params=_params(("parallel",)),
    )(*h_parts, o, hnorm_w.reshape(1, w))


def _rope_tile(y, cos, sin_signed):
    lane = lax.broadcasted_iota(jnp.int32, y.shape, 1)
    partner = jnp.where(lane % ROPE_AXIS_DIM < ROPE_HALF,
                        pltpu.roll(y, LANES - ROPE_HALF, 1), pltpu.roll(y, ROPE_HALF, 1))
    return y * cos + partner * sin_signed


def _mla_lat_kernel(*refs, rope):
    if rope:
        a_ref, qw_ref, kvw_ref, krw_ref, cos_ref, sin_ref, cq_ref, ckv_ref, kr_ref, kr2_ref = refs
    else:
        a_ref, qw_ref, kvw_ref, krw_ref, cq_ref, ckv_ref, kr_ref, kr2_ref = refs
    cq = a_ref[:, :MLA_Q_LORA]
    cq_ref[...] = (cq * lax.rsqrt(jnp.mean(cq * cq, axis=-1, keepdims=True) + EPS) * qw_ref[...]).astype(BF16)
    ckv = a_ref[:, MLA_Q_LORA:MLA_Q_LORA + MLA_KV_LORA]
    ckv_ref[...] = ckv * lax.rsqrt(jnp.mean(ckv * ckv, axis=-1, keepdims=True) + EPS) * kvw_ref[...]
    kr = a_ref[:, MLA_Q_LORA + MLA_KV_LORA:MLA_Q_LORA + MLA_KV_LORA + LANES]
    kr = jnp.where(lax.broadcasted_iota(jnp.int32, kr.shape, 1) < MLA_ROPE, kr, 0.0)
    ms = jnp.sum(kr * kr, axis=-1, keepdims=True) * (1.0 / MLA_ROPE)
    krn = kr * lax.rsqrt(ms + EPS) * krw_ref[...]
    kr_ref[...] = krn
    kr2 = krn + pltpu.roll(krn, MLA_ROPE, 1)
    if rope:
        kr2 = _rope_tile(kr2, cos_ref[...], sin_ref[...])
    kr2_ref[...] = kr2.astype(BF16)


def mla_latents(a, qnorm_w, kvnorm_w, kr_w, rope_tables):
    m = a.shape[0]
    tm = 256
    rope = rope_tables is not None
    in_specs = [
        pl.BlockSpec((tm, MLA_IN_PAD), lambda i: (i, 0)),
        pl.BlockSpec((1, MLA_Q_LORA), lambda i: (0, 0)),
        pl.BlockSpec((1, MLA_KV_LORA), lambda i: (0, 0)),
        pl.BlockSpec((1, LANES), lambda i: (0, 0)),
    ]
    args = [a, qnorm_w.reshape(1, -1), kvnorm_w.reshape(1, -1), kr_w]
    if rope:
        nt = rope_tables[0].shape[0] // tm
        in_specs += [pl.BlockSpec((tm, LANES), lambda i: (i % nt, 0))] * 2
        args += list(rope_tables)
    return pl.pallas_call(
        functools.partial(_mla_lat_kernel, rope=rope),
        grid=(m // tm,),
        in_specs=in_specs,
        out_specs=[
            pl.BlockSpec((tm, MLA_Q_LORA), lambda i: (i, 0)),
            pl.BlockSpec((tm, MLA_KV_LORA), lambda i: (i, 0)),
            pl.BlockSpec((tm, LANES), lambda i: (i, 0)),
            pl.BlockSpec((tm, LANES), lambda i: (i, 0)),
        ],
        out_shape=[
            jax.ShapeDtypeStruct((m, MLA_Q_LORA), BF16),
            jax.ShapeDtypeStruct((m, MLA_KV_LORA), F32),
            jax.ShapeDtypeStruct((m, LANES), F32),
            jax.ShapeDtypeStruct((m, LANES), BF16),
        ],
        name="mla_latents",
        compiler_params=_params(("parallel",)),
    )(*args)


PAIR_W = 2 * MLA_QK_DIM


def _regroup_head_pairs(w_uq):
    nl, k, _ = w_uq.shape
    w = w_uq.reshape(nl, k, MLA_HEADS // 2, 2, MLA_QK_DIM)
    nope = w[..., :MLA_NOPE].reshape(nl, k, MLA_HEADS // 2, 2 * MLA_NOPE)
    rope = w[..., MLA_NOPE:].reshape(nl, k, MLA_HEADS // 2, 2 * MLA_ROPE)
    return jnp.concatenate([nope, rope], axis=-1).reshape(nl, k, MLA_HEADS * MLA_QK_DIM)
PAIRS_PER_TILE = 2


def _mla_q_kernel(*refs, rope):
    if rope:
        x_ref, w_ref, nw_ref, rw_ref, cos_ref, sin_ref, o_ref = refs
    else:
        x_ref, w_ref, nw_ref, rw_ref, o_ref = refs
    acc = jnp.dot(x_ref[...], w_ref[...].astype(BF16), preferred_element_type=F32)
    lane = lax.broadcasted_iota(jnp.int32, (acc.shape[0], LANES), 1)
    first = lane < MLA_ROPE
    for p in range(PAIRS_PER_TILE):
        base = p * PAIR_W
        obase = p * 2 * MLA_HEAD_PAD
        for hh in range(2):
            x = acc[:, base + hh * MLA_NOPE: base + (hh + 1) * MLA_NOPE]
            y = x * lax.rsqrt(jnp.mean(x * x, axis=-1, keepdims=True) + EPS) * nw_ref[...]
            o_ref[:, obase + hh * MLA_HEAD_PAD: obase + hh * MLA_HEAD_PAD + MLA_NOPE] = y.astype(BF16)
        r = acc[:, base + 2 * MLA_NOPE: base + PAIR_W]
        r2 = r * r
        ss_a = jnp.sum(jnp.where(first, r2, 0.0), axis=-1, keepdims=True)
        ss_b = jnp.sum(jnp.where(first, 0.0, r2), axis=-1, keepdims=True)
        inv = jnp.where(first, lax.rsqrt(ss_a * (1.0 / MLA_ROPE) + EPS), lax.rsqrt(ss_b * (1.0 / MLA_ROPE) + EPS))
        rn = r * inv * rw_ref[...]
        if rope:
            rn = _rope_tile(rn, cos_ref[...], sin_ref[...])
        o_ref[:, obase + MLA_NOPE: obase + MLA_HEAD_PAD] = jnp.where(first, rn, 0.0).astype(BF16)
        o_ref[:, obase + MLA_HEAD_PAD + MLA_NOPE: obase + 2 * MLA_HEAD_PAD] = jnp.where(first, 0.0, rn).astype(BF16)


def mla_queries(cqn, w_uq, q_nope_w, q_rope_w2, rope_tables, *, layer):
    m, k = cqn.shape
    tm = 1024
    tn = PAIRS_PER_TILE * PAIR_W
    to = PAIRS_PER_TILE * 2 * MLA_HEAD_PAD
    rope = rope_tables is not None
    in_specs = [
        pl.BlockSpec((tm, k), lambda i, j: (i, 0)),
        pl.BlockSpec((None, k, tn), lambda i, j: (layer, 0, j)),
        pl.BlockSpec((1, MLA_NOPE), lambda i, j: (0, 0)),
        pl.BlockSpec((1, LANES), lambda i, j: (0, 0)),
    ]
    args = [cqn, w_uq, q_nope_w, q_rope_w2]
    if rope:
        nt = rope_tables[0].shape[0] // tm
        in_specs += [pl.BlockSpec((tm, LANES), lambda i, j: (i % nt, 0))] * 2
        args += list(rope_tables)
    return pl.pallas_call(
        functools.partial(_mla_q_kernel, rope=rope),
        grid=(m // tm, w_uq.shape[2] // tn),
        in_specs=in_specs,
        out_specs=pl.BlockSpec((tm, to), lambda i, j: (i, j)),
        out_shape=jax.ShapeDtypeStruct((m, MLA_HEADS * MLA_HEAD_PAD), BF16),
        name="mla_queries",
        compiler_params=_params(("parallel", "parallel")),
    )(*args)


KV_HEADS_PER_TILE = 4


def _mla_kv_kernel(x_ref, w_ref, kr2_ref, nw_ref, k_ref, v_ref):
    acc = jnp.dot(x_ref[...], w_ref[...].astype(BF16), preferred_element_type=F32)
    kr2 = kr2_ref[...]
    for hh in range(KV_HEADS_PER_TILE):
        base = hh * (MLA_NOPE + MLA_V)
        x = acc[:, base: base + MLA_NOPE]
        y = x * lax.rsqrt(jnp.mean(x * x, axis=-1, keepdims=True) + EPS) * nw_ref[...]
        k_ref[:, hh * MLA_HEAD_PAD: hh * MLA_HEAD_PAD + MLA_NOPE] = y.astype(BF16)
        k_ref[:, hh * MLA_HEAD_PAD + MLA_NOPE: (hh + 1) * MLA_HEAD_PAD] = kr2
        v_ref[:, hh * MLA_V: (hh + 1) * MLA_V] = acc[:, base + MLA_NOPE: base + MLA_NOPE + MLA_V].astype(BF16)


def mla_keys_values(ckv, kr2, w_ukv, k_nope_w, *, layer):
    m, k = ckv.shape
    tm = 512
    tn = KV_HEADS_PER_TILE * (MLA_NOPE + MLA_V)
    return pl.pallas_call(
        _mla_kv_kernel,
        grid=(m // tm, w_ukv.shape[2] // tn),
        in_specs=[
            pl.BlockSpec((tm, k), lambda i, j: (i, 0)),
            pl.BlockSpec((None, k, tn), lambda i, j: (layer, 0, j)),
            pl.BlockSpec((tm, LANES), lambda i, j: (i, 0)),
            pl.BlockSpec((1, MLA_NOPE), lambda i, j: (0, 0)),
        ],
        out_specs=[
            pl.BlockSpec((tm, KV_HEADS_PER_TILE * MLA_HEAD_PAD), lambda i, j: (i, j)),
            pl.BlockSpec((tm, KV_HEADS_PER_TILE * MLA_V), lambda i, j: (i, j)),
        ],
        out_shape=[
            jax.ShapeDtypeStruct((m, MLA_HEADS * MLA_HEAD_PAD), BF16),
            jax.ShapeDtypeStruct((m, MLA_HEADS * MLA_V), BF16),
        ],
        name="mla_keys_values",
        compiler_params=_params(("parallel", "parallel")),
    )(ckv, w_ukv, kr2, k_nope_w)


def _attn_kernel(q_ref, k_ref, v_ref, o_ref, *, heads):
    exp2_scale = float(MLA_QK_DIM ** -0.5 * np.log2(np.e))
    for hh in range(heads):
        q = q_ref[:, hh * MLA_HEAD_PAD:(hh + 1) * MLA_HEAD_PAD]
        k = k_ref[:, hh * MLA_HEAD_PAD:(hh + 1) * MLA_HEAD_PAD]
        s = lax.dot_general(q, k, NT_DIMS, preferred_element_type=F32)
        p = jnp.exp2((s - jnp.max(s, axis=-1, keepdims=True)) * exp2_scale)
        o = jnp.dot(p.astype(BF16), v_ref[:, hh * MLA_V:(hh + 1) * MLA_V], preferred_element_type=F32)
        o_ref[:, hh * MLA_V:(hh + 1) * MLA_V] = (o / jnp.sum(p, axis=-1, keepdims=True)).astype(BF16)


def attention(q, k, v, *, batch, heads_per_step, tq):
    m = q.shape[0]
    t = m // batch
    s = k.shape[0] // batch
    nq = t // tq
    hb = heads_per_step
    return pl.pallas_call(
        functools.partial(_attn_kernel, heads=hb),
        grid=(batch, MLA_HEADS // hb, nq),
        in_specs=[
            pl.BlockSpec((tq, hb * MLA_HEAD_PAD), lambda b, g, i: (b * nq + i, g)),
            pl.BlockSpec((s, hb * MLA_HEAD_PAD), lambda b, g, i: (b, g)),
            pl.BlockSpec((s, hb * MLA_V), lambda b, g, i: (b, g)),
        ],
        out_specs=pl.BlockSpec((tq, hb * MLA_V), lambda b, g, i: (b * nq + i, g)),
        out_shape=jax.ShapeDtypeStruct((m, MLA_HEADS * MLA_V), BF16),
        name="attention",
        compiler_params=_params(("parallel", "parallel", "parallel"), VMEM_LIMIT_BIG),
    )(q, k, v)


def _rope_tables(n_tok):
    pos = np.arange(n_tok)
    row = (pos // GRID_W).astype(np.float32)
    col = (pos % GRID_W).astype(np.float32)
    inv_freq = jnp.asarray(ROPE_BASE, F32) ** (-jnp.arange(0, ROPE_AXIS_DIM, 2, dtype=F32) / ROPE_AXIS_DIM)
    lane = np.arange(LANES)
    freq = inv_freq[lane % ROPE_HALF]
    use_row = (lane % MLA_ROPE) < ROPE_AXIS_DIM
    ang = jnp.where(use_row[None, :], row[:, None], col[:, None]) * freq[None, :]
    sign = np.where(lane % ROPE_AXIS_DIM < ROPE_HALF, -1.0, 1.0).astype(np.float32)
    return jnp.cos(ang), jnp.sin(ang) * sign[None, :]


def _mlstm_layer(h, x, mod, w_in_t, gate_b, hnorm_w, w_out, *, layer, batch, state, emit_state, mod_kw):
    tn = 512
    u_qkv = matmul(h, w_in_t, layer=layer, w_is_nk=True, n_out=2 * MLSTM_QK_W + MLSTM_V_W, tn=tn, out_dtype=BF16)
    o = matmul(h, w_in_t, layer=layer, w_is_nk=True, n_out=MLSTM_V_W, tn=tn,
               col_off=(2 * MLSTM_QK_W + MLSTM_V_W) // tn, out_dtype=F32)
    g, gt = mlstm_gates(h, w_in_t, gate_b[layer], layer=layer)
    outs = mlstm_core(u_qkv, g, gt, batch=batch, state=state, emit_state=emit_state)
    n_h = len(outs) - (3 if emit_state else 0)
    hg = mlstm_hnorm_gate(outs[:n_h], o, hnorm_w[layer])
    x = matmul(hg, w_out, layer=layer, n_out=D_MODEL, tn=256, out_dtype=F32, epilogue="resid", res=x, mod=mod,
               gate_piece=2, **mod_kw)
    return x, outs[n_h:]


def _mla_layer(h, x, mod, wts, *, layer, batch, ctx, rope_tables, mod_kw):
    w_in_t, qnorm_w, kvnorm_w, w_uq, w_ukv, q_nope_w, q_rope_w2, k_nope_w, kr_w, w_out = wts
    a = matmul(h, w_in_t, layer=layer, w_is_nk=True, n_out=MLA_IN_PAD, tn=512, out_dtype=F32)
    cqn, ckvn, krn, kr2 = mla_latents(a, qnorm_w, kvnorm_w, kr_w, rope_tables)
    q = mla_queries(cqn, w_uq, q_nope_w, q_rope_w2, rope_tables, layer=layer)
    ckv_all, kr2_all = ckvn.astype(BF16), kr2
    if ctx is not None:
        ckv_ctx, kr_ctx = ctx
        t = ckvn.shape[0] // batch
        kr2_ctx = jnp.concatenate([kr_ctx, kr_ctx], axis=-1).astype(BF16)
        ckv_all = jnp.concatenate([ckv_ctx.astype(BF16), ckv_all.reshape(batch, t, -1)], axis=1).reshape(-1, MLA_KV_LORA)
        kr2_all = jnp.concatenate([kr2_ctx, kr2.reshape(batch, t, -1)], axis=1).reshape(-1, LANES)
    k, v = mla_keys_values(ckv_all, kr2_all, w_ukv, k_nope_w, layer=layer)
    if ctx is None:
        o = attention(q, k, v, batch=batch, heads_per_step=8, tq=q.shape[0] // batch)
    else:
        o = attention(q, k, v, batch=batch, heads_per_step=8, tq=512)
    x = matmul(o, w_out, layer=layer, n_out=D_MODEL, tn=256, out_dtype=F32, epilogue="resid", res=x, mod=mod,
               gate_piece=2, **mod_kw)
    return x, ckvn, krn


def _mlp(x, mod, norm_w, w1, w2, *, layer, mod_kw):
    h = norm_modulate(x, norm_w[layer], mod, 3, **mod_kw)
    u = matmul(h, w1, layer=layer, n_out=D_FF, tn=512, out_dtype=BF16, epilogue="relu2")
    return matmul_ksplit_resid(u, w2, x, mod, layer=layer, gate_piece=5, **mod_kw)


def kernel(x_prompt, x_sample, state_mlstm_C, state_mlstm_n, state_mlstm_m, cache_mla_ckv, cache_mla_krope, c, c_ctx,
           norm1_w, norm2_w, mod_w, mod_b, mlp_w1, mlp_w2,
           mlstm_w_in, mlstm_gate_b, mlstm_hnorm_w, mlstm_w_out,
           mla_w_in, mla_qnorm_w, mla_kvnorm_w, mla_w_uq, mla_w_ukv, mla_q_norm_w, mla_k_norm_w, mla_w_out):
    bp, tp, d = x_prompt.shape
    bs, ts, _ = x_sample.shape
    xp = x_prompt.reshape(bp * tp, d)
    xs = x_sample.reshape(bs * ts, d)
    cond = jnp.zeros((MOD_ROWS, d), F32).at[0].set(c_ctx).at[1:1 + bs].set(c)
    mod = modulation_table(cond, mod_w, mod_b)
    kw_p = dict(rows_per_mod=bp * tp, mod_base=0)
    kw_s = dict(rows_per_mod=ts, mod_base=1)

    wa = (jnp.swapaxes(mlstm_w_in, 1, 2), mlstm_gate_b, mlstm_hnorm_w, mlstm_w_out)
    hp = norm_modulate(xp, norm1_w[0], mod[0], 0, **kw_p)
    hs = norm_modulate(xs, norm1_w[0], mod[0], 0, **kw_s)
    xp, (c_new, n_new, m_new) = _mlstm_layer(hp, xp, mod[0], *wa, layer=0, batch=bp, state=None, emit_state=True,
                                             mod_kw=kw_p)
    state = (state_mlstm_C[:, 0], state_mlstm_n[:, 0], state_mlstm_m[:, 0])
    xs, _ = _mlstm_layer(hs, xs, mod[0], *wa, layer=0, batch=bs, state=state, emit_state=False, mod_kw=kw_s)
    xp = _mlp(xp, mod[0], norm2_w, mlp_w1, mlp_w2, layer=0, mod_kw=kw_p)
    xs = _mlp(xs, mod[0], norm2_w, mlp_w1, mlp_w2, layer=0, mod_kw=kw_s)

    qw, kw = mla_q_norm_w[0], mla_k_norm_w[0]
    pad = lambda w: jnp.concatenate([w, jnp.zeros_like(w)]).reshape(1, LANES)
    wb = (
        jnp.swapaxes(mla_w_in, 1, 2), mla_qnorm_w[0], mla_kvnorm_w[0], _regroup_head_pairs(mla_w_uq), mla_w_ukv,
        qw[:MLA_NOPE].reshape(1, -1), jnp.concatenate([qw[MLA_NOPE:], qw[MLA_NOPE:]]).reshape(1, LANES),
        kw[:MLA_NOPE].reshape(1, -1), pad(kw[MLA_NOPE:]), mla_w_out,
    )
    hp = norm_modulate(xp, norm1_w[1], mod[1], 0, **kw_p)
    hs = norm_modulate(xs, norm1_w[1], mod[1], 0, **kw_s)
    xp, ckv_p, kr_p = _mla_layer(hp, xp, mod[1], wb, layer=0, batch=bp, ctx=None, rope_tables=None, mod_kw=kw_p)
    xs, _, _ = _mla_layer(hs, xs, mod[1], wb, layer=0, batch=bs, ctx=(cache_mla_ckv[:, 0], cache_mla_krope[:, 0]),
                          rope_tables=_rope_tables(ts), mod_kw=kw_s)
    xp = _mlp(xp, mod[1], norm2_w, mlp_w1, mlp_w2, layer=1, mod_kw=kw_p)
    xs = _mlp(xs, mod[1], norm2_w, mlp_w1, mlp_w2, layer=1, mod_kw=kw_s)

    return (
        xp.reshape(bp, tp, d),
        xs.reshape(bs, ts, d),
        c_new.reshape(bp, 1, 2, MLSTM_HEADS, MLSTM_DK, MLSTM_DV),
        n_new.reshape(bp, 1, 2, MLSTM_HEADS, MLSTM_DK),
        m_new[..., 0, 0].reshape(bp, 1, 2, MLSTM_HEADS),
        ckv_p.reshape(bp, 1, tp, MLA_KV_LORA),
        kr_p[:, :MLA_ROPE].reshape(bp, 1, tp, MLA_ROPE),
    )
```

```python
import functools

import jax
import jax.numpy as jnp
import numpy as np
from jax import lax
from jax.experimental import pallas as pl
from jax.experimental.pallas import tpu as pltpu

F32 = jnp.float32
BF16 = jnp.bfloat16

D_MODEL = 4096
D_FF = 4 * D_MODEL
N_MOD = 6
EPS = 1e-6
GRID_W = 64

MLSTM_HEADS = 8
MLSTM_DK = 256
MLSTM_DV = 512
MLSTM_QK_W = MLSTM_HEADS * MLSTM_DK
MLSTM_V_W = MLSTM_HEADS * MLSTM_DV
MLSTM_N_GATES = 4 * MLSTM_HEADS
MLSTM_CHUNK = 256

MLA_HEADS = 32
MLA_NOPE = 128
MLA_ROPE = 64
MLA_V = 128
MLA_QK_DIM = MLA_NOPE + MLA_ROPE
MLA_Q_LORA = 1024
MLA_KV_LORA = 512
MLA_IN_PAD = 2048
MLA_HEAD_PAD = 256
ROPE_AXIS_DIM = MLA_ROPE // 2
ROPE_HALF = ROPE_AXIS_DIM // 2
ROPE_BASE = 10000.0

LANES = 128
SUBLANES = 8
MOD_ROWS = SUBLANES
VMEM_LIMIT_BIG = 58 * 1024 * 1024
VMEM_LIMIT_MED = 40 * 1024 * 1024

NT_DIMS = (((1,), (1,)), ((), ()))
TN_DIMS = (((0,), (0,)), ((), ()))


def _params(semantics, vmem=VMEM_LIMIT_MED):
    return pltpu.CompilerParams(dimension_semantics=semantics, vmem_limit_bytes=vmem)


def _mod_row(tile, tm, rows_per_mod, mod_base):
    return mod_base + (tile * tm) // rows_per_mod


def _mod_kernel(c_ref, w_ref, b_ref, o_ref):
    c = c_ref[...]
    a = (c * (1.0 / (1.0 + jnp.exp(-c)))).astype(BF16)
    o_ref[0] = jnp.dot(a, w_ref[0].astype(BF16), preferred_element_type=F32) + b_ref[0]


def modulation_table(cond, mod_w, mod_b):
    depth, d, n = mod_w.shape
    tn = 1024
    return pl.pallas_call(
        _mod_kernel,
        grid=(depth, n // tn),
        in_specs=[
            pl.BlockSpec((MOD_ROWS, d), lambda l, j: (0, 0)),
            pl.BlockSpec((1, d, tn), lambda l, j: (l, 0, j)),
            pl.BlockSpec((1, 1, tn), lambda l, j: (l, 0, j)),
        ],
        out_specs=pl.BlockSpec((1, MOD_ROWS, tn), lambda l, j: (l, 0, j)),
        out_shape=jax.ShapeDtypeStruct((depth, MOD_ROWS, n), F32),
        name="modulation_table",
        compiler_params=_params(("parallel", "parallel"), VMEM_LIMIT_BIG),
    )(cond, mod_w, mod_b.reshape(depth, 1, n))


def _normmod_kernel(x_ref, nw_ref, shift_ref, scale_ref, o_ref, *, tm, rows_per_mod, mod_base):
    x = x_ref[...]
    row = _mod_row(pl.program_id(0), tm, rows_per_mod, mod_base)
    shift = shift_ref[pl.ds(row, 1), :]
    gain = nw_ref[...] * (1.0 + scale_ref[pl.ds(row, 1), :])
    y = x * lax.rsqrt(jnp.mean(x * x, axis=-1, keepdims=True) + EPS)
    o_ref[...] = (y * gain + shift).astype(BF16)


def norm_modulate(x, norm_w, mod, piece, *, rows_per_mod, mod_base):
    m, d = x.shape
    tm = 256
    return pl.pallas_call(
        functools.partial(_normmod_kernel, tm=tm, rows_per_mod=rows_per_mod, mod_base=mod_base),
        grid=(m // tm,),
        in_specs=[
            pl.BlockSpec((tm, d), lambda i: (i, 0)),
            pl.BlockSpec((1, d), lambda i: (0, 0)),
            pl.BlockSpec((MOD_ROWS, d), lambda i: (0, piece)),
            pl.BlockSpec((MOD_ROWS, d), lambda i: (0, piece + 1)),
        ],
        out_specs=pl.BlockSpec((tm, d), lambda i: (i, 0)),
        out_shape=jax.ShapeDtypeStruct((m, d), BF16),
        name="norm_modulate",
        compiler_params=_params(("parallel",)),
    )(x, norm_w.reshape(1, d), mod, mod)


def _mm_kernel(*refs, epilogue, w_is_nk, tm, rows_per_mod, mod_base):
    if epilogue == "resid":
        x_ref, w_ref, res_ref, gate_ref, o_ref = refs
    else:
        x_ref, w_ref, o_ref = refs
    if w_is_nk:
        acc = lax.dot_general(x_ref[...], w_ref[...].astype(BF16), NT_DIMS, preferred_element_type=F32)
    else:
        acc = jnp.dot(x_ref[...], w_ref[...].astype(BF16), preferred_element_type=F32)
    if epilogue == "relu2":
        r = jnp.maximum(acc, 0.0)
        o_ref[...] = (r * r).astype(o_ref.dtype)
    elif epilogue == "resid":
        row = _mod_row(pl.program_id(0), tm, rows_per_mod, mod_base)
        o_ref[...] = res_ref[...] + gate_ref[pl.ds(row, 1), :] * acc
    else:
        o_ref[...] = acc.astype(o_ref.dtype)


def matmul(x, w, *, n_out, tn, out_dtype, layer=0, col_off=0, w_is_nk=False, epilogue=None, res=None, mod=None,
           gate_piece=0, rows_per_mod=1, mod_base=0, tm=2048):
    m, k = x.shape
    if w_is_nk:
        w_spec = pl.BlockSpec((None, tn, k), lambda i, j: (layer, j + col_off, 0))
    else:
        w_spec = pl.BlockSpec((None, k, tn), lambda i, j: (layer, 0, j + col_off))
    in_specs = [pl.BlockSpec((tm, k), lambda i, j: (i, 0), pipeline_mode=pl.Buffered(1)), w_spec]
    args = [x, w]
    if epilogue == "resid":
        gate_off = gate_piece * (D_MODEL // tn)
        in_specs += [
            pl.BlockSpec((tm, tn), lambda i, j: (i, j)),
            pl.BlockSpec((MOD_ROWS, tn), lambda i, j: (0, gate_off + j)),
        ]
        args += [res, mod]
    return pl.pallas_call(
        functools.partial(_mm_kernel, epilogue=epilogue, w_is_nk=w_is_nk, tm=tm, rows_per_mod=rows_per_mod,
                          mod_base=mod_base),
        grid=(m // tm, n_out // tn),
        in_specs=in_specs,
        out_specs=pl.BlockSpec((tm, tn), lambda i, j: (i, j)),
        out_shape=jax.ShapeDtypeStruct((m, n_out), out_dtype),
        name=f"matmul_{epilogue or 'plain'}",
        compiler_params=_params(("parallel", "parallel"), VMEM_LIMIT_BIG),
    )(*args)


def _mm_split_kernel(x_ref, w_ref, o1_ref, o2_ref, *, n1):
    j = pl.program_id(1)

    def prod():
        return lax.dot_general(x_ref[...], w_ref[...].astype(BF16), NT_DIMS, preferred_element_type=F32)

    @pl.when(j < n1)
    def _():
        o1_ref[...] = prod().astype(o1_ref.dtype)

    @pl.when(j >= n1)
    def _():
        o2_ref[...] = prod().astype(o2_ref.dtype)


def matmul_split(x, w_t, *, layer, n_out1, dtype1, n_out2, dtype2, tn, tm=2048):
    m, k = x.shape
    n1, n2 = n_out1 // tn, n_out2 // tn
    return pl.pallas_call(
        functools.partial(_mm_split_kernel, n1=n1),
        grid=(m // tm, n1 + n2),
        in_specs=[
            pl.BlockSpec((tm, k), lambda i, j: (i, 0), pipeline_mode=pl.Buffered(1)),
            pl.BlockSpec((None, tn, k), lambda i, j: (layer, j, 0)),
        ],
        out_specs=[
            pl.BlockSpec((tm, tn), lambda i, j: (i, jnp.minimum(j, n1 - 1))),
            pl.BlockSpec((tm, tn), lambda i, j: (i, jnp.maximum(j - n1, 0))),
        ],
        out_shape=[jax.ShapeDtypeStruct((m, n_out1), dtype1), jax.ShapeDtypeStruct((m, n_out2), dtype2)],
        name="matmul_split",
        compiler_params=_params(("parallel", "arbitrary"), VMEM_LIMIT_BIG),
    )(x, w_t)


def _mm_ksplit_kernel(x_ref, w_ref, res_ref, gate_ref, o_ref, *, nk, tm, rows_per_mod, mod_base):
    k = pl.program_id(2)
    row = _mod_row(pl.program_id(0), tm, rows_per_mod, mod_base)

    def part():
        return gate_ref[pl.ds(row, 1), :] * jnp.dot(x_ref[...], w_ref[...].astype(BF16),
                                                    preferred_element_type=F32)

    @pl.when(k == 0)
    def _():
        o_ref[...] = part()

    @pl.when(k > 0)
    def _():
        o_ref[...] = o_ref[...] + part()

    slab = tm // nk
    rows = pl.ds(pl.multiple_of(k * slab, slab), slab)
    o_ref[rows, :] = o_ref[rows, :] + res_ref[...]


def matmul_ksplit_resid(x, w, res, mod, *, layer, gate_piece, rows_per_mod, mod_base, tm=2048, tn=1024, tk=1024):
    m, kdim = x.shape
    n = w.shape[2]
    nk = kdim // tk
    gate_off = gate_piece * (D_MODEL // tn)
    return pl.pallas_call(
        functools.partial(_mm_ksplit_kernel, nk=nk, tm=tm, rows_per_mod=rows_per_mod, mod_base=mod_base),
        grid=(m // tm, n // tn, nk),
        in_specs=[
            pl.BlockSpec((tm, tk), lambda i, j, k: (i, k)),
            pl.BlockSpec((None, tk, tn), lambda i, j, k: (layer, k, j)),
            pl.BlockSpec((tm // nk, tn), lambda i, j, k: (i * nk + k, j)),
            pl.BlockSpec((MOD_ROWS, tn), lambda i, j, k: (0, gate_off + j)),
        ],
        out_specs=pl.BlockSpec((tm, tn), lambda i, j, k: (i, j)),
        out_shape=jax.ShapeDtypeStruct((m, n), F32),
        name="matmul_ksplit_resid",
        compiler_params=_params(("parallel", "parallel", "arbitrary"), VMEM_LIMIT_BIG),
    )(x, w, res, mod)


def _log_sigmoid(x):
    return jnp.minimum(x, 0.0) - jnp.log(1.0 + jnp.exp(-jnp.abs(x)))


def _gate_kernel(h_ref, wg_ref, b_ref, g_ref, gt_ref):
    gt = lax.dot_general(wg_ref[...].astype(BF16), h_ref[...], NT_DIMS, preferred_element_type=F32)
    row = lax.broadcasted_iota(jnp.int32, gt.shape, 0)
    gt = jnp.where(row < MLSTM_N_GATES, gt, 0.0) + b_ref[...]
    gt = jnp.where((row // MLSTM_HEADS) % 2 == 1, _log_sigmoid(gt), gt)
    gt_ref[...] = gt[:MLSTM_N_GATES, :]
    g_ref[...] = gt.T[:, :MLSTM_N_GATES]


def mlstm_gates(h, w_in_t, gate_b, *, layer):
    m, d = h.shape
    ng = MLSTM_N_GATES
    tm = 1024
    gate_blk = (2 * MLSTM_QK_W + 2 * MLSTM_V_W) // LANES
    bias = jnp.zeros((LANES, 1), F32).at[:ng, 0].set(gate_b)
    return pl.pallas_call(
        _gate_kernel,
        grid=(m // tm,),
        in_specs=[
            pl.BlockSpec((tm, d), lambda i: (i, 0)),
            pl.BlockSpec((None, LANES, d), lambda i: (layer, gate_blk, 0)),
            pl.BlockSpec((LANES, 1), lambda i: (0, 0)),
        ],
        out_specs=[pl.BlockSpec((tm, ng), lambda i: (i, 0)), pl.BlockSpec((ng, tm), lambda i: (0, i))],
        out_shape=[jax.ShapeDtypeStruct((m, ng), F32), jax.ShapeDtypeStruct((ng, m), F32)],
        name="mlstm_gates",
        compiler_params=_params(("parallel",)),
    )(h, w_in_t, bias)


def _mlstm_direction(q, k, v, i_col, f_col, i_row, f_row, c_ref, n_ref, m_ref, *, causal, has_prev):
    L = q.shape[0]
    t_idx = lax.broadcasted_iota(jnp.int32, (L, L), 0)
    s_idx = lax.broadcasted_iota(jnp.int32, (L, L), 1)
    if causal:
        mask, mask_t = s_idx <= t_idx, t_idx <= s_idx
    else:
        mask, mask_t = s_idx >= t_idx, t_idx >= s_idx
    b_col = jnp.sum(jnp.where(mask, f_row, 0.0), axis=1, keepdims=True)
    b_row = jnp.sum(jnp.where(mask_t, f_col, 0.0), axis=0, keepdims=True)
    d = jnp.where(mask, b_col - b_row + i_row, -jnp.inf)
    m_prev = m_ref[...] if has_prev else jnp.zeros((1, 1), F32)
    m_inter = b_col + m_prev
    m_t = jnp.maximum(m_inter, jnp.max(d, axis=1, keepdims=True))
    s = lax.dot_general(q, k, NT_DIMS, preferred_element_type=F32) * jnp.exp(d - m_t)
    num = jnp.dot(s.astype(BF16), v, preferred_element_type=F32)
    den = jnp.sum(s, axis=1, keepdims=True)
    if has_prev:
        a_inter = jnp.exp(m_inter - m_t)
        num = num + a_inter * jnp.dot(q, c_ref[...].astype(BF16), preferred_element_type=F32)
        den = den + a_inter * jnp.sum(q.astype(F32) * n_ref[...], axis=1, keepdims=True)
    h = num / jnp.maximum(jnp.abs(den), jnp.exp(-m_t))

    b_last = jnp.sum(f_col, axis=0, keepdims=True)
    w_col = b_last - b_col + i_col
    m_new = jnp.maximum(b_last + m_prev, jnp.max(w_col, axis=0, keepdims=True))
    wk = jnp.exp(w_col - m_new) * k.astype(F32)
    c_upd = lax.dot_general(wk.astype(BF16), v, TN_DIMS, preferred_element_type=F32)
    n_upd = jnp.sum(wk, axis=0, keepdims=True)
    if has_prev:
        decay = jnp.exp(b_last + m_prev - m_new)
        c_ref[...] = decay * c_ref[...] + c_upd
        n_ref[...] = decay * n_ref[...] + n_upd
    else:
        c_ref[...] = c_upd
        n_ref[...] = n_upd
    m_ref[...] = m_new
    return h


def _mlstm_kernel(*refs, nc, hps, has_state, emit_state):
    refs = list(refs)
    m0_ref = refs.pop(0) if has_state else None
    qf_ref, kf_ref, vf_ref, gf_ref, gtf_ref = refs[:5]
    refs = refs[5:]
    if nc > 1:
        qb_ref, kb_ref, vb_ref, gb_ref, gtb_ref = refs[:5]
        refs = refs[5:]
    else:
        qb_ref, kb_ref, vb_ref, gb_ref, gtb_ref = qf_ref, kf_ref, vf_ref, gf_ref, gtf_ref
    if has_state:
        c0_ref, n0_ref = refs[:2]
        refs = refs[2:]
    if nc > 1:
        hf_ref, hb_ref = refs[:2]
        refs = refs[2:]
    else:
        hs_ref = refs.pop(0)
    if emit_state:
        cout_ref, nout_ref, mout_ref = refs[:3]
        refs = refs[3:]
    c_sc, n_sc, m_sc = refs

    b = pl.program_id(0)
    head0 = pl.program_id(1) * hps
    c = pl.program_id(2)
    has_prev = has_state or nc > 1
    dk, dv = MLSTM_DK, MLSTM_DV

    if has_prev:
        @pl.when(c == 0)
        def _():
            for dr in range(2):
                for j in range(hps):
                    if has_state:
                        c_sc[dr, j] = c0_ref[0, dr, j]
                        n_sc[dr, j] = n0_ref[0, dr, j]
                        m_sc[dr, j] = jnp.full((1, 1), m0_ref[(b * 2 + dr) * MLSTM_HEADS + head0 + j], F32)
                    else:
                        c_sc[dr, j] = jnp.zeros((dk, dv), F32)
                        n_sc[dr, j] = jnp.zeros((1, dk), F32)
                        m_sc[dr, j] = jnp.zeros((1, 1), F32)

    def gates(g_ref, gt_ref, idx, j):
        g = g_ref[...]
        lane = lax.broadcasted_iota(jnp.int32, g.shape, 1)
        sel = idx * MLSTM_HEADS + head0 + j
        col = jnp.sum(jnp.where(lane == sel, g, 0.0), axis=1, keepdims=True)
        return col, gt_ref[pl.ds(sel, 1), :]

    q_scale = MLSTM_DK ** -0.5
    for j in range(hps):
        outs = []
        for dr, (q_ref, k_ref, v_ref, g_ref, gt_ref) in enumerate(
                ((qf_ref, kf_ref, vf_ref, gf_ref, gtf_ref), (qb_ref, kb_ref, vb_ref, gb_ref, gtb_ref))):
            i_col, i_row = gates(g_ref, gt_ref, 2 * dr, j)
            f_col, f_row = gates(g_ref, gt_ref, 2 * dr + 1, j)
            outs.append(_mlstm_direction(
                q_ref[:, j * dk:(j + 1) * dk] * q_scale, k_ref[:, j * dk:(j + 1) * dk], v_ref[:, j * dv:(j + 1) * dv],
                i_col, f_col, i_row, f_row, c_sc.at[dr, j], n_sc.at[dr, j], m_sc.at[dr, j],
                causal=(dr == 0), has_prev=has_prev))
        if nc > 1:
            hf_ref[:, j * dv:(j + 1) * dv] = outs[0]
            hb_ref[:, j * dv:(j + 1) * dv] = outs[1]
        else:
            hs_ref[:, j * dv:(j + 1) * dv] = outs[0] + outs[1]

    if emit_state:
        @pl.when(c == nc - 1)
        def _():
            for dr in range(2):
                for j in range(hps):
                    cout_ref[0, dr, j] = c_sc[dr, j]
                    nout_ref[0, dr, j] = n_sc[dr, j]
                    mout_ref[0, dr, j] = jnp.broadcast_to(m_sc[dr, j], (1, LANES))


def mlstm_core(u_qkv, g, gt, *, batch, heads_per_step, state=None, emit_state=False):
    m = u_qkv.shape[0]
    L = MLSTM_CHUNK
    nc = m // batch // L
    hps = heads_per_step
    nh, dk, dv = MLSTM_HEADS, hps * MLSTM_DK, hps * MLSTM_DV
    k_blk0 = MLSTM_QK_W // dk
    v_blk0 = 2 * MLSTM_QK_W // dv
    has_state = state is not None

    def specs(chunk_of):
        rb = lambda b, h, c: b * nc + chunk_of(c)
        return [
            pl.BlockSpec((L, dk), lambda b, h, c: (rb(b, h, c), h)),
            pl.BlockSpec((L, dk), lambda b, h, c: (rb(b, h, c), k_blk0 + h)),
            pl.BlockSpec((L, dv), lambda b, h, c: (rb(b, h, c), v_blk0 + h)),
            pl.BlockSpec((L, MLSTM_N_GATES), lambda b, h, c: (rb(b, h, c), 0)),
            pl.BlockSpec((MLSTM_N_GATES, L), lambda b, h, c: (0, rb(b, h, c))),
        ]

    fwd_chunk = lambda c: c
    bwd_chunk = lambda c: nc - 1 - c
    in_specs, args = [], []
    if has_state:
        c0, n0, m0 = state
        in_specs.append(pl.BlockSpec(memory_space=pltpu.SMEM))
        args.append(m0.reshape(-1))
    in_specs += specs(fwd_chunk)
    args += [u_qkv, u_qkv, u_qkv, g, gt]
    if nc > 1:
        in_specs += specs(bwd_chunk)
        args += [u_qkv, u_qkv, u_qkv, g, gt]
    state_specs = [
        pl.BlockSpec((1, 2, hps, MLSTM_DK, MLSTM_DV), lambda b, h, c: (b, 0, h, 0, 0)),
        pl.BlockSpec((1, 2, hps, 1, MLSTM_DK), lambda b, h, c: (b, 0, h, 0, 0)),
    ]
    if has_state:
        in_specs += state_specs
        args += [c0, n0.reshape(batch, 2, nh, 1, MLSTM_DK)]

    h_shape = jax.ShapeDtypeStruct((m, MLSTM_V_W), F32)
    if nc > 1:
        out_specs = [
            pl.BlockSpec((L, dv), lambda b, h, c: (b * nc + fwd_chunk(c), h)),
            pl.BlockSpec((L, dv), lambda b, h, c: (b * nc + bwd_chunk(c), h)),
        ]
        out_shape = [h_shape, h_shape]
    else:
        out_specs = [pl.BlockSpec((L, dv), lambda b, h, c: (b, h))]
        out_shape = [h_shape]
    if emit_state:
        out_specs += state_specs + [pl.BlockSpec((1, 2, hps, 1, LANES), lambda b, h, c: (b, 0, h, 0, 0))]
        out_shape += [
            jax.ShapeDtypeStruct((batch, 2, nh, MLSTM_DK, MLSTM_DV), F32),
            jax.ShapeDtypeStruct((batch, 2, nh, 1, MLSTM_DK), F32),
            jax.ShapeDtypeStruct((batch, 2, nh, 1, LANES), F32),
        ]
    return pl.pallas_call(
        functools.partial(_mlstm_kernel, nc=nc, hps=hps, has_state=has_state, emit_state=emit_state),
        grid=(batch, nh // hps, nc),
        in_specs=in_specs,
        out_specs=out_specs,
        out_shape=out_shape,
        scratch_shapes=[pltpu.VMEM((2, hps, MLSTM_DK, MLSTM_DV), F32), pltpu.VMEM((2, hps, 1, MLSTM_DK), F32),
                        pltpu.VMEM((2, hps, 1, 1), F32)],
        name="mlstm_core",
        compiler_params=_params(("parallel", "parallel", "arbitrary"), VMEM_LIMIT_BIG),
    )(*args)


def _hnorm_kernel(*refs, n_in):
    h_refs, (o_ref, w_ref, out_ref) = refs[:n_in], refs[n_in:]
    for hd in range(MLSTM_HEADS):
        sl = slice(hd * MLSTM_DV, (hd + 1) * MLSTM_DV)
        x = h_refs[0][:, sl]
        for r in h_refs[1:]:
            x = x + r[:, sl]
        y = x * lax.rsqrt(jnp.mean(x * x, axis=-1, keepdims=True) + EPS) * w_ref[:, sl]
        gate = 0.5 * jnp.tanh(0.5 * o_ref[:, sl]) + 0.5
        out_ref[:, sl] = (y * gate).astype(BF16)


def mlstm_hnorm_gate(h_parts, o, hnorm_w):
    m, w = o.shape
    tm = 256
    spec = pl.BlockSpec((tm, w), lambda i: (i, 0))
    return pl.pallas_call(
        functools.partial(_hnorm_kernel, n_in=len(h_parts)),
        grid=(m // tm,),
        in_specs=[spec] * len(h_parts) + [spec, pl.BlockSpec((1, w), lambda i: (0, 0))],
        out_specs=spec,
        out_shape=jax.ShapeDtypeStruct((m, w), BF16),
        name="mlstm_hnorm_gate",
        compiler_params=_params(("parallel",)),
    )(*h_parts, o, hnorm_w.reshape(1, w))


def _rope_tile(y, cos, sin_signed):
    lane = lax.broadcasted_iota(jnp.int32, y.shape, 1)
    partner = jnp.where(lane % ROPE_AXIS_DIM < ROPE_HALF,
                        pltpu.roll(y, LANES - ROPE_HALF, 1), pltpu.roll(y, ROPE_HALF, 1))
    return y * cos + partner * sin_signed


def _mla_lat_kernel(*refs, rope):
    if rope:
        a_ref, qw_ref, kvw_ref, krw_ref, cos_ref, sin_ref, cq_ref, ckv_ref, kr_ref, kr2_ref = refs
    else:
        a_ref, qw_ref, kvw_ref, krw_ref, cq_ref, ckv_ref, kr_ref, kr2_ref = refs
    cq = a_ref[:, :MLA_Q_LORA]
    cq_ref[...] = (cq * lax.rsqrt(jnp.mean(cq * cq, axis=-1, keepdims=True) + EPS) * qw_ref[...]).astype(BF16)
    ckv = a_ref[:, MLA_Q_LORA:MLA_Q_LORA + MLA_KV_LORA]
    ckv_ref[...] = ckv * lax.rsqrt(jnp.mean(ckv * ckv, axis=-1, keepdims=True) + EPS) * kvw_ref[...]
    kr = a_ref[:, MLA_Q_LORA + MLA_KV_LORA:MLA_Q_LORA + MLA_KV_LORA + LANES]
    kr = jnp.where(lax.broadcasted_iota(jnp.int32, kr.shape, 1) < MLA_ROPE, kr, 0.0)
    ms = jnp.sum(kr * kr, axis=-1, keepdims=True) * (1.0 / MLA_ROPE)
    krn = kr * lax.rsqrt(ms + EPS) * krw_ref[...]
    kr_ref[...] = krn
    kr2 = krn + pltpu.roll(krn, MLA_ROPE, 1)
    if rope:
        kr2 = _rope_tile(kr2, cos_ref[...], sin_ref[...])
    kr2_ref[...] = kr2.astype(BF16)


def mla_latents(a, qnorm_w, kvnorm_w, kr_w, rope_tables):
    m = a.shape[0]
    tm = 256
    rope = rope_tables is not None
    in_specs = [
        pl.BlockSpec((tm, MLA_IN_PAD), lambda i: (i, 0)),
        pl.BlockSpec((1, MLA_Q_LORA), lambda i: (0, 0)),
        pl.BlockSpec((1, MLA_KV_LORA), lambda i: (0, 0)),
        pl.BlockSpec((1, LANES), lambda i: (0, 0)),
    ]
    args = [a, qnorm_w.reshape(1, -1), kvnorm_w.reshape(1, -1), kr_w]
    if rope:
        nt = rope_tables[0].shape[0] // tm
        in_specs += [pl.BlockSpec((tm, LANES), lambda i: (i % nt, 0))] * 2
        args += list(rope_tables)
    return pl.pallas_call(
        functools.partial(_mla_lat_kernel, rope=rope),
        grid=(m // tm,),
        in_specs=in_specs,
        out_specs=[
            pl.BlockSpec((tm, MLA_Q_LORA), lambda i: (i, 0)),
            pl.BlockSpec((tm, MLA_KV_LORA), lambda i: (i, 0)),
            pl.BlockSpec((tm, LANES), lambda i: (i, 0)),
            pl.BlockSpec((tm, LANES), lambda i: (i, 0)),
        ],
        out_shape=[
            jax.ShapeDtypeStruct((m, MLA_Q_LORA), BF16),
            jax.ShapeDtypeStruct((m, MLA_KV_LORA), F32),
            jax.ShapeDtypeStruct((m, LANES), F32),
            jax.ShapeDtypeStruct((m, LANES), BF16),
        ],
        name="mla_latents",
        compiler_params=_params(("parallel",)),
    )(*args)


PAIR_W = 2 * MLA_QK_DIM


def _regroup_head_pairs(w_uq):
    nl, k, _ = w_uq.shape
    w = w_uq.reshape(nl, k, MLA_HEADS // 2, 2, MLA_QK_DIM)
    nope = w[..., :MLA_NOPE].reshape(nl, k, MLA_HEADS // 2, 2 * MLA_NOPE)
    rope = w[..., MLA_NOPE:].reshape(nl, k, MLA_HEADS // 2, 2 * MLA_ROPE)
    return jnp.concatenate([nope, rope], axis=-1).reshape(nl, k, MLA_HEADS * MLA_QK_DIM)
PAIRS_PER_TILE = 2


def _mla_q_kernel(*refs, rope):
    if rope:
        x_ref, w_ref, nw_ref, rw_ref, cos_ref, sin_ref, o_ref = refs
    else:
        x_ref, w_ref, nw_ref, rw_ref, o_ref = refs
    acc = jnp.dot(x_ref[...], w_ref[...].astype(BF16), preferred_element_type=F32)
    lane = lax.broadcasted_iota(jnp.int32, (acc.shape[0], LANES), 1)
    first = lane < MLA_ROPE
    for p in range(PAIRS_PER_TILE):
        base = p * PAIR_W
        obase = p * 2 * MLA_HEAD_PAD
        for hh in range(2):
            x = acc[:, base + hh * MLA_NOPE: base + (hh + 1) * MLA_NOPE]
            y = x * lax.rsqrt(jnp.mean(x * x, axis=-1, keepdims=True) + EPS) * nw_ref[...]
            o_ref[:, obase + hh * MLA_HEAD_PAD: obase + hh * MLA_HEAD_PAD + MLA_NOPE] = y.astype(BF16)
        r = acc[:, base + 2 * MLA_NOPE: base + PAIR_W]
        r2 = r * r
        ss_a = jnp.sum(jnp.where(first, r2, 0.0), axis=-1, keepdims=True)
        ss_b = jnp.sum(jnp.where(first, 0.0, r2), axis=-1, keepdims=True)
        inv = jnp.where(first, lax.rsqrt(ss_a * (1.0 / MLA_ROPE) + EPS), lax.rsqrt(ss_b * (1.0 / MLA_ROPE) + EPS))
        rn = r * inv * rw_ref[...]
        if rope:
            rn = _rope_tile(rn, cos_ref[...], sin_ref[...])
        o_ref[:, obase + MLA_NOPE: obase + MLA_HEAD_PAD] = jnp.where(first, rn, 0.0).astype(BF16)
        o_ref[:, obase + MLA_HEAD_PAD + MLA_NOPE: obase + 2 * MLA_HEAD_PAD] = jnp.where(first, 0.0, rn).astype(BF16)


def mla_queries(cqn, w_uq, q_nope_w, q_rope_w2, rope_tables, *, layer):
    m, k = cqn.shape
    tm = 1024
    tn = PAIRS_PER_TILE * PAIR_W
    to = PAIRS_PER_TILE * 2 * MLA_HEAD_PAD
    rope = rope_tables is not None
    in_specs = [
        pl.BlockSpec((tm, k), lambda i, j: (i, 0)),
        pl.BlockSpec((None, k, tn), lambda i, j: (layer, 0, j)),
        pl.BlockSpec((1, MLA_NOPE), lambda i, j: (0, 0)),
        pl.BlockSpec((1, LANES), lambda i, j: (0, 0)),
    ]
    args = [cqn, w_uq, q_nope_w, q_rope_w2]
    if rope:
        nt = rope_tables[0].shape[0] // tm
        in_specs += [pl.BlockSpec((tm, LANES), lambda i, j: (i % nt, 0))] * 2
        args += list(rope_tables)
    return pl.pallas_call(
        functools.partial(_mla_q_kernel, rope=rope),
        grid=(m // tm, w_uq.shape[2] // tn),
        in_specs=in_specs,
        out_specs=pl.BlockSpec((tm, to), lambda i, j: (i, j)),
        out_shape=jax.ShapeDtypeStruct((m, MLA_HEADS * MLA_HEAD_PAD), BF16),
        name="mla_queries",
        compiler_params=_params(("parallel", "parallel")),
    )(*args)


KV_HEADS_PER_TILE = 8


def _mla_kv_kernel(x_ref, w_ref, kr2_ref, nw_ref, k_ref, v_ref):
    acc = jnp.dot(x_ref[...], w_ref[...].astype(BF16), preferred_element_type=F32)
    kr2 = kr2_ref[...]
    for hh in range(KV_HEADS_PER_TILE):
        base = hh * (MLA_NOPE + MLA_V)
        x = acc[:, base: base + MLA_NOPE]
        y = x * lax.rsqrt(jnp.mean(x * x, axis=-1, keepdims=True) + EPS) * nw_ref[...]
        k_ref[:, hh * MLA_HEAD_PAD: hh * MLA_HEAD_PAD + MLA_NOPE] = y.astype(BF16)
        k_ref[:, hh * MLA_HEAD_PAD + MLA_NOPE: (hh + 1) * MLA_HEAD_PAD] = kr2
        v_ref[:, hh * MLA_V: (hh + 1) * MLA_V] = acc[:, base + MLA_NOPE: base + MLA_NOPE + MLA_V].astype(BF16)


def mla_keys_values(ckv, kr2, w_ukv, k_nope_w, *, layer, tm):
    m, k = ckv.shape
    tn = KV_HEADS_PER_TILE * (MLA_NOPE + MLA_V)
    return pl.pallas_call(
        _mla_kv_kernel,
        grid=(m // tm, w_ukv.shape[2] // tn),
        in_specs=[
            pl.BlockSpec((tm, k), lambda i, j: (i, 0)),
            pl.BlockSpec((None, k, tn), lambda i, j: (layer, 0, j)),
            pl.BlockSpec((tm, LANES), lambda i, j: (i, 0)),
            pl.BlockSpec((1, MLA_NOPE), lambda i, j: (0, 0)),
        ],
        out_specs=[
            pl.BlockSpec((tm, KV_HEADS_PER_TILE * MLA_HEAD_PAD), lambda i, j: (i, j)),
            pl.BlockSpec((tm, KV_HEADS_PER_TILE * MLA_V), lambda i, j: (i, j)),
        ],
        out_shape=[
            jax.ShapeDtypeStruct((m, MLA_HEADS * MLA_HEAD_PAD), BF16),
            jax.ShapeDtypeStruct((m, MLA_HEADS * MLA_V), BF16),
        ],
        name="mla_keys_values",
        compiler_params=_params(("parallel", "parallel"), VMEM_LIMIT_BIG),
    )(ckv, w_ukv, kr2, k_nope_w)


def _attn_kernel(q_ref, k_ref, v_ref, o_ref, *, heads):
    exp2_scale = float(MLA_QK_DIM ** -0.5 * np.log2(np.e))
    for hh in range(heads):
        q = q_ref[:, hh * MLA_HEAD_PAD:(hh + 1) * MLA_HEAD_PAD]
        k = k_ref[:, hh * MLA_HEAD_PAD:(hh + 1) * MLA_HEAD_PAD]
        s = lax.dot_general(q, k, NT_DIMS, preferred_element_type=F32)
        p = jnp.exp2((s - jnp.max(s, axis=-1, keepdims=True)) * exp2_scale)
        o = jnp.dot(p.astype(BF16), v_ref[:, hh * MLA_V:(hh + 1) * MLA_V], preferred_element_type=F32)
        o_ref[:, hh * MLA_V:(hh + 1) * MLA_V] = (o / jnp.sum(p, axis=-1, keepdims=True)).astype(BF16)


def attention(q, k, v, *, batch, heads_per_step, tq):
    m = q.shape[0]
    t = m // batch
    s = k.shape[0] // batch
    nq = t // tq
    hb = heads_per_step
    return pl.pallas_call(
        functools.partial(_attn_kernel, heads=hb),
        grid=(batch, MLA_HEADS // hb, nq),
        in_specs=[
            pl.BlockSpec((tq, hb * MLA_HEAD_PAD), lambda b, g, i: (b * nq + i, g)),
            pl.BlockSpec((s, hb * MLA_HEAD_PAD), lambda b, g, i: (b, g)),
            pl.BlockSpec((s, hb * MLA_V), lambda b, g, i: (b, g)),
        ],
        out_specs=pl.BlockSpec((tq, hb * MLA_V), lambda b, g, i: (b * nq + i, g)),
        out_shape=jax.ShapeDtypeStruct((m, MLA_HEADS * MLA_V), BF16),
        name="attention",
        compiler_params=_params(("parallel", "parallel", "parallel"), VMEM_LIMIT_BIG),
    )(q, k, v)


def _rope_tables(n_tok):
    pos = np.arange(n_tok)
    row = (pos // GRID_W).astype(np.float32)
    col = (pos % GRID_W).astype(np.float32)
    inv_freq = jnp.asarray(ROPE_BASE, F32) ** (-jnp.arange(0, ROPE_AXIS_DIM, 2, dtype=F32) / ROPE_AXIS_DIM)
    lane = np.arange(LANES)
    freq = inv_freq[lane % ROPE_HALF]
    use_row = (lane % MLA_ROPE) < ROPE_AXIS_DIM
    ang = jnp.where(use_row[None, :], row[:, None], col[:, None]) * freq[None, :]
    sign = np.where(lane % ROPE_AXIS_DIM < ROPE_HALF, -1.0, 1.0).astype(np.float32)
    return jnp.cos(ang), jnp.sin(ang) * sign[None, :]


def _mlstm_layer(h, x, mod, w_in_t, gate_b, hnorm_w, w_out, *, layer, batch, state, emit_state, mod_kw):
    tn = 512
    u_qkv, o = matmul_split(h, w_in_t, layer=layer, n_out1=2 * MLSTM_QK_W + MLSTM_V_W, dtype1=BF16,
                            n_out2=MLSTM_V_W, dtype2=F32, tn=tn)
    g, gt = mlstm_gates(h, w_in_t, gate_b[layer], layer=layer)
    outs = mlstm_core(u_qkv, g, gt, batch=batch, state=state, emit_state=emit_state,
                      heads_per_step=MLSTM_HEADS if state is None else MLSTM_HEADS // 2)
    n_h = len(outs) - (3 if emit_state else 0)
    hg = mlstm_hnorm_gate(outs[:n_h], o, hnorm_w[layer])
    x = matmul(hg, w_out, layer=layer, n_out=D_MODEL, tn=512, out_dtype=F32, epilogue="resid", res=x, mod=mod,
               gate_piece=2, **mod_kw)
    return x, outs[n_h:]


def _mla_layer(h, x, mod, wts, *, layer, batch, ctx, rope_tables, mod_kw):
    w_in_t, qnorm_w, kvnorm_w, w_uq, w_ukv, q_nope_w, q_rope_w2, k_nope_w, kr_w, w_out = wts
    a = matmul(h, w_in_t, layer=layer, w_is_nk=True, n_out=MLA_IN_PAD, tn=512, out_dtype=F32)
    cqn, ckvn, krn, kr2 = mla_latents(a, qnorm_w, kvnorm_w, kr_w, rope_tables)
    q = mla_queries(cqn, w_uq, q_nope_w, q_rope_w2, rope_tables, layer=layer)
    ckv_all, kr2_all = ckvn.astype(BF16), kr2
    if ctx is not None:
        ckv_ctx, kr_ctx = ctx
        t = ckvn.shape[0] // batch
        kr2_ctx = jnp.concatenate([kr_ctx, kr_ctx], axis=-1).astype(BF16)
        ckv_all = jnp.concatenate([ckv_ctx.astype(BF16), ckv_all.reshape(batch, t, -1)], axis=1).reshape(-1, MLA_KV_LORA)
        kr2_all = jnp.concatenate([kr2_ctx, kr2.reshape(batch, t, -1)], axis=1).reshape(-1, LANES)
    k, v = mla_keys_values(ckv_all, kr2_all, w_ukv, k_nope_w, layer=layer, tm=1024 if ctx is None else 1536)
    if ctx is None:
        o = attention(q, k, v, batch=batch, heads_per_step=8, tq=q.shape[0] // batch)
    else:
        o = attention(q, k, v, batch=batch, heads_per_step=8, tq=512)
    x = matmul(o, w_out, layer=layer, n_out=D_MODEL, tn=512, out_dtype=F32, epilogue="resid", res=x, mod=mod,
               gate_piece=2, **mod_kw)
    return x, ckvn, krn


def _mlp(x, mod, norm_w, w1, w2, *, layer, mod_kw):
    h = norm_modulate(x, norm_w[layer], mod, 3, **mod_kw)
    u = matmul(h, w1, layer=layer, n_out=D_FF, tn=512, out_dtype=BF16, epilogue="relu2")
    return matmul_ksplit_resid(u, w2, x, mod, layer=layer, gate_piece=5, **mod_kw)


def kernel(x_prompt, x_sample, state_mlstm_C, state_mlstm_n, state_mlstm_m, cache_mla_ckv, cache_mla_krope, c, c_ctx,
           norm1_w, norm2_w, mod_w, mod_b, mlp_w1, mlp_w2,
           mlstm_w_in, mlstm_gate_b, mlstm_hnorm_w, mlstm_w_out,
           mla_w_in, mla_qnorm_w, mla_kvnorm_w, mla_w_uq, mla_w_ukv, mla_q_norm_w, mla_k_norm_w, mla_w_out):
    bp, tp, d = x_prompt.shape
    bs, ts, _ = x_sample.shape
    xp = x_prompt.reshape(bp * tp, d)
    xs = x_sample.reshape(bs * ts, d)
    cond = jnp.zeros((MOD_ROWS, d), F32).at[0].set(c_ctx).at[1:1 + bs].set(c)
    mod = modulation_table(cond, mod_w, mod_b)
    kw_p = dict(rows_per_mod=bp * tp, mod_base=0)
    kw_s = dict(rows_per_mod=ts, mod_base=1)

    wa = (jnp.swapaxes(mlstm_w_in, 1, 2), mlstm_gate_b, mlstm_hnorm_w, mlstm_w_out)
    hp = norm_modulate(xp, norm1_w[0], mod[0], 0, **kw_p)
    hs = norm_modulate(xs, norm1_w[0], mod[0], 0, **kw_s)
    xp, (c_new, n_new, m_new) = _mlstm_layer(hp, xp, mod[0], *wa, layer=0, batch=bp, state=None, emit_state=True,
                                             mod_kw=kw_p)
    state = (state_mlstm_C[:, 0], state_mlstm_n[:, 0], state_mlstm_m[:, 0])
    xs, _ = _mlstm_layer(hs, xs, mod[0], *wa, layer=0, batch=bs, state=state, emit_state=False, mod_kw=kw_s)
    xp = _mlp(xp, mod[0], norm2_w, mlp_w1, mlp_w2, layer=0, mod_kw=kw_p)
    xs = _mlp(xs, mod[0], norm2_w, mlp_w1, mlp_w2, layer=0, mod_kw=kw_s)

    qw, kw = mla_q_norm_w[0], mla_k_norm_w[0]
    pad = lambda w: jnp.concatenate([w, jnp.zeros_like(w)]).reshape(1, LANES)
    wb = (
        jnp.swapaxes(mla_w_in, 1, 2), mla_qnorm_w[0], mla_kvnorm_w[0], _regroup_head_pairs(mla_w_uq), mla_w_ukv,
        qw[:MLA_NOPE].reshape(1, -1), jnp.concatenate([qw[MLA_NOPE:], qw[MLA_NOPE:]]).reshape(1, LANES),
        kw[:MLA_NOPE].reshape(1, -1), pad(kw[MLA_NOPE:]), mla_w_out,
    )
    hp = norm_modulate(xp, norm1_w[1], mod[1], 0, **kw_p)
    hs = norm_modulate(xs, norm1_w[1], mod[1], 0, **kw_s)
    xp, ckv_p, kr_p = _mla_layer(hp, xp, mod[1], wb, layer=0, batch=bp, ctx=None, rope_tables=None, mod_kw=kw_p)
    xs, _, _ = _mla_layer(hs, xs, mod[1], wb, layer=0, batch=bs, ctx=(cache_mla_ckv[:, 0], cache_mla_krope[:, 0]),
                          rope_tables=_rope_tables(ts), mod_kw=kw_s)
    xp = _mlp(xp, mod[1], norm2_w, mlp_w1, mlp_w2, layer=1, mod_kw=kw_p)
    xs = _mlp(xs, mod[1], norm2_w, mlp_w1, mlp_w2, layer=1, mod_kw=kw_s)

    return (
        xp.reshape(bp, tp, d),
        xs.reshape(bs, ts, d),
        c_new.reshape(bp, 1, 2, MLSTM_HEADS, MLSTM_DK, MLSTM_DV),
        n_new.reshape(bp, 1, 2, MLSTM_HEADS, MLSTM_DK),
        m_new[..., 0, 0].reshape(bp, 1, 2, MLSTM_HEADS),
        ckv_p.reshape(bp, 1, tp, MLA_KV_LORA),
        kr_p[:, :MLA_ROPE].reshape(bp, 1, tp, MLA_ROPE),
    )
```

```python
import functools

import jax
import jax.numpy as jnp
import numpy as np
from jax import lax
from jax.experimental import pallas as pl
from jax.experimental.pallas import tpu as pltpu

F32 = jnp.float32
BF16 = jnp.bfloat16

D_MODEL = 4096
D_FF = 4 * D_MODEL
N_MOD = 6
EPS = 1e-6
GRID_W = 64

MLSTM_HEADS = 8
MLSTM_DK = 256
MLSTM_DV = 512
MLSTM_QK_W = MLSTM_HEADS * MLSTM_DK
MLSTM_V_W = MLSTM_HEADS * MLSTM_DV
MLSTM_N_GATES = 4 * MLSTM_HEADS
MLSTM_CHUNK = 256

MLA_HEADS = 32
MLA_NOPE = 128
MLA_ROPE = 64
MLA_V = 128
MLA_QK_DIM = MLA_NOPE + MLA_ROPE
MLA_Q_LORA = 1024
MLA_KV_LORA = 512
MLA_IN_PAD = 2048
MLA_HEAD_PAD = 256
ROPE_AXIS_DIM = MLA_ROPE // 2
ROPE_HALF = ROPE_AXIS_DIM // 2
ROPE_BASE = 10000.0

LANES = 128
SUBLANES = 8
MOD_ROWS = SUBLANES
VMEM_LIMIT_BIG = 58 * 1024 * 1024
VMEM_LIMIT_MED = 40 * 1024 * 1024

NT_DIMS = (((1,), (1,)), ((), ()))
TN_DIMS = (((0,), (0,)), ((), ()))


def _params(semantics, vmem=VMEM_LIMIT_MED):
    return pltpu.CompilerParams(dimension_semantics=semantics, vmem_limit_bytes=vmem)


def _mod_row(tile, tm, rows_per_mod, mod_base):
    return mod_base + (tile * tm) // rows_per_mod


def _mod_kernel(c_ref, w_ref, b_ref, o_ref):
    c = c_ref[...]
    a = (c * (1.0 / (1.0 + jnp.exp(-c)))).astype(BF16)
    o_ref[0] = jnp.dot(a, w_ref[0].astype(BF16), preferred_element_type=F32) + b_ref[0]


def modulation_table(cond, mod_w, mod_b):
    depth, d, n = mod_w.shape
    tn = 1024
    return pl.pallas_call(
        _mod_kernel,
        grid=(depth, n // tn),
        in_specs=[
            pl.BlockSpec((MOD_ROWS, d), lambda l, j: (0, 0)),
            pl.BlockSpec((1, d, tn), lambda l, j: (l, 0, j)),
            pl.BlockSpec((1, 1, tn), lambda l, j: (l, 0, j)),
        ],
        out_specs=pl.BlockSpec((1, MOD_ROWS, tn), lambda l, j: (l, 0, j)),
        out_shape=jax.ShapeDtypeStruct((depth, MOD_ROWS, n), F32),
        name="modulation_table",
        compiler_params=_params(("parallel", "parallel"), VMEM_LIMIT_BIG),
    )(cond, mod_w, mod_b.reshape(depth, 1, n))


def _normmod_kernel(x_ref, nw_ref, shift_ref, scale_ref, o_ref, *, tm, rows_per_mod, mod_base):
    x = x_ref[...]
    row = _mod_row(pl.program_id(0), tm, rows_per_mod, mod_base)
    shift = shift_ref[pl.ds(row, 1), :]
    gain = nw_ref[...] * (1.0 + scale_ref[pl.ds(row, 1), :])
    y = x * lax.rsqrt(jnp.mean(x * x, axis=-1, keepdims=True) + EPS)
    o_ref[...] = (y * gain + shift).astype(BF16)


def norm_modulate(x, norm_w, mod, piece, *, rows_per_mod, mod_base):
    m, d = x.shape
    tm = 512
    return pl.pallas_call(
        functools.partial(_normmod_kernel, tm=tm, rows_per_mod=rows_per_mod, mod_base=mod_base),
        grid=(m // tm,),
        in_specs=[
            pl.BlockSpec((tm, d), lambda i: (i, 0)),
            pl.BlockSpec((1, d), lambda i: (0, 0)),
            pl.BlockSpec((MOD_ROWS, d), lambda i: (0, piece)),
            pl.BlockSpec((MOD_ROWS, d), lambda i: (0, piece + 1)),
        ],
        out_specs=pl.BlockSpec((tm, d), lambda i: (i, 0)),
        out_shape=jax.ShapeDtypeStruct((m, d), BF16),
        name="norm_modulate",
        compiler_params=_params(("parallel",)),
    )(x, norm_w.reshape(1, d), mod, mod)


def _mm_kernel(*refs, epilogue, w_is_nk, tm, rows_per_mod, mod_base):
    if epilogue == "resid":
        x_ref, w_ref, res_ref, gate_ref, o_ref = refs
    else:
        x_ref, w_ref, o_ref = refs
    if w_is_nk:
        acc = lax.dot_general(x_ref[...], w_ref[...].astype(BF16), NT_DIMS, preferred_element_type=F32)
    else:
        acc = jnp.dot(x_ref[...], w_ref[...].astype(BF16), preferred_element_type=F32)
    if epilogue == "relu2":
        r = jnp.maximum(acc, 0.0)
        o_ref[...] = (r * r).astype(o_ref.dtype)
    elif epilogue == "resid":
        row = _mod_row(pl.program_id(0), tm, rows_per_mod, mod_base)
        o_ref[...] = res_ref[...] + gate_ref[pl.ds(row, 1), :] * acc
    else:
        o_ref[...] = acc.astype(o_ref.dtype)


def matmul(x, w, *, n_out, tn, out_dtype, layer=0, col_off=0, w_is_nk=False, epilogue=None, res=None, mod=None,
           gate_piece=0, rows_per_mod=1, mod_base=0, tm=2048):
    m, k = x.shape
    if w_is_nk:
        w_spec = pl.BlockSpec((None, tn, k), lambda i, j: (layer, j + col_off, 0))
    else:
        w_spec = pl.BlockSpec((None, k, tn), lambda i, j: (layer, 0, j + col_off))
    in_specs = [pl.BlockSpec((tm, k), lambda i, j: (i, 0), pipeline_mode=pl.Buffered(1)), w_spec]
    args = [x, w]
    if epilogue == "resid":
        gate_off = gate_piece * (D_MODEL // tn)
        in_specs += [
            pl.BlockSpec((tm, tn), lambda i, j: (i, j)),
            pl.BlockSpec((MOD_ROWS, tn), lambda i, j: (0, gate_off + j)),
        ]
        args += [res, mod]
    return pl.pallas_call(
        functools.partial(_mm_kernel, epilogue=epilogue, w_is_nk=w_is_nk, tm=tm, rows_per_mod=rows_per_mod,
                          mod_base=mod_base),
        grid=(m // tm, n_out // tn),
        in_specs=in_specs,
        out_specs=pl.BlockSpec((tm, tn), lambda i, j: (i, j)),
        out_shape=jax.ShapeDtypeStruct((m, n_out), out_dtype),
        name=f"matmul_{epilogue or 'plain'}",
        compiler_params=_params(("parallel", "parallel"), VMEM_LIMIT_BIG),
    )(*args)


def _mm_split_kernel(x_ref, w_ref, o1_ref, o2_ref, *, n1):
    j = pl.program_id(1)

    def prod():
        return lax.dot_general(x_ref[...], w_ref[...].astype(BF16), NT_DIMS, preferred_element_type=F32)

    @pl.when(j < n1)
    def _():
        o1_ref[...] = prod().astype(o1_ref.dtype)

    @pl.when(j >= n1)
    def _():
        o2_ref[...] = prod().astype(o2_ref.dtype)


def matmul_split(x, w_t, *, layer, n_out1, dtype1, n_out2, dtype2, tn, tm=2048):
    m, k = x.shape
    n1, n2 = n_out1 // tn, n_out2 // tn
    return pl.pallas_call(
        functools.partial(_mm_split_kernel, n1=n1),
        grid=(m // tm, n1 + n2),
        in_specs=[
            pl.BlockSpec((tm, k), lambda i, j: (i, 0), pipeline_mode=pl.Buffered(1)),
            pl.BlockSpec((None, tn, k), lambda i, j: (layer, j, 0)),
        ],
        out_specs=[
            pl.BlockSpec((tm, tn), lambda i, j: (i, jnp.minimum(j, n1 - 1))),
            pl.BlockSpec((tm, tn), lambda i, j: (i, jnp.maximum(j - n1, 0))),
        ],
        out_shape=[jax.ShapeDtypeStruct((m, n_out1), dtype1), jax.ShapeDtypeStruct((m, n_out2), dtype2)],
        name="matmul_split",
        compiler_params=_params(("parallel", "arbitrary"), VMEM_LIMIT_BIG),
    )(x, w_t)


def _mm_ksplit_kernel(x_ref, w_ref, res_ref, gate_ref, o_ref, *, nk, tm, rows_per_mod, mod_base):
    k = pl.program_id(2)
    row = _mod_row(pl.program_id(0), tm, rows_per_mod, mod_base)

    def part():
        return gate_ref[pl.ds(row, 1), :] * jnp.dot(x_ref[...], w_ref[...].astype(BF16),
                                                    preferred_element_type=F32)

    @pl.when(k == 0)
    def _():
        o_ref[...] = part()

    @pl.when(k > 0)
    def _():
        o_ref[...] = o_ref[...] + part()

    slab = tm // nk
    rows = pl.ds(pl.multiple_of(k * slab, slab), slab)
    o_ref[rows, :] = o_ref[rows, :] + res_ref[...]


def matmul_ksplit_resid(x, w, res, mod, *, layer, gate_piece, rows_per_mod, mod_base, tm=2048, tn=1024, tk=2048):
    m, kdim = x.shape
    n = w.shape[2]
    nk = kdim // tk
    gate_off = gate_piece * (D_MODEL // tn)
    return pl.pallas_call(
        functools.partial(_mm_ksplit_kernel, nk=nk, tm=tm, rows_per_mod=rows_per_mod, mod_base=mod_base),
        grid=(m // tm, n // tn, nk),
        in_specs=[
            pl.BlockSpec((tm, tk), lambda i, j, k: (i, k)),
            pl.BlockSpec((None, tk, tn), lambda i, j, k: (layer, k, j)),
            pl.BlockSpec((tm // nk, tn), lambda i, j, k: (i * nk + k, j)),
            pl.BlockSpec((MOD_ROWS, tn), lambda i, j, k: (0, gate_off + j)),
        ],
        out_specs=pl.BlockSpec((tm, tn), lambda i, j, k: (i, j), pipeline_mode=pl.Buffered(1)),
        out_shape=jax.ShapeDtypeStruct((m, n), F32),
        name="matmul_ksplit_resid",
        compiler_params=_params(("parallel", "parallel", "arbitrary"), VMEM_LIMIT_BIG),
    )(x, w, res, mod)


def _log_sigmoid(x):
    return jnp.minimum(x, 0.0) - jnp.log(1.0 + jnp.exp(-jnp.abs(x)))


def _gate_kernel(h_ref, wg_ref, b_ref, g_ref, gt_ref):
    gt = lax.dot_general(wg_ref[...].astype(BF16), h_ref[...], NT_DIMS, preferred_element_type=F32)
    row = lax.broadcasted_iota(jnp.int32, gt.shape, 0)
    gt = jnp.where(row < MLSTM_N_GATES, gt, 0.0) + b_ref[...]
    gt = jnp.where((row // MLSTM_HEADS) % 2 == 1, _log_sigmoid(gt), gt)
    gt_ref[...] = gt[:MLSTM_N_GATES, :]
    g_ref[...] = gt.T[:, :MLSTM_N_GATES]


def mlstm_gates(h, w_in_t, gate_b, *, layer):
    m, d = h.shape
    ng = MLSTM_N_GATES
    tm = 1024
    gate_blk = (2 * MLSTM_QK_W + 2 * MLSTM_V_W) // LANES
    bias = jnp.zeros((LANES, 1), F32).at[:ng, 0].set(gate_b)
    return pl.pallas_call(
        _gate_kernel,
        grid=(m // tm,),
        in_specs=[
            pl.BlockSpec((tm, d), lambda i: (i, 0)),
            pl.BlockSpec((None, LANES, d), lambda i: (layer, gate_blk, 0)),
            pl.BlockSpec((LANES, 1), lambda i: (0, 0)),
        ],
        out_specs=[pl.BlockSpec((tm, ng), lambda i: (i, 0)), pl.BlockSpec((ng, tm), lambda i: (0, i))],
        out_shape=[jax.ShapeDtypeStruct((m, ng), F32), jax.ShapeDtypeStruct((ng, m), F32)],
        name="mlstm_gates",
        compiler_params=_params(("parallel",)),
    )(h, w_in_t, bias)


def _mlstm_direction(q, k, v, i_col, f_col, i_row, f_row, c_ref, n_ref, m_ref, *, causal, has_prev):
    L = q.shape[0]
    t_idx = lax.broadcasted_iota(jnp.int32, (L, L), 0)
    s_idx = lax.broadcasted_iota(jnp.int32, (L, L), 1)
    if causal:
        mask, mask_t = s_idx <= t_idx, t_idx <= s_idx
    else:
        mask, mask_t = s_idx >= t_idx, t_idx >= s_idx
    b_col = jnp.sum(jnp.where(mask, f_row, 0.0), axis=1, keepdims=True)
    b_row = jnp.sum(jnp.where(mask_t, f_col, 0.0), axis=0, keepdims=True)
    d = jnp.where(mask, b_col - b_row + i_row, -jnp.inf)
    m_prev = m_ref[...] if has_prev else jnp.zeros((1, 1), F32)
    m_inter = b_col + m_prev
    m_t = jnp.maximum(m_inter, jnp.max(d, axis=1, keepdims=True))
    s = lax.dot_general(q, k, NT_DIMS, preferred_element_type=F32) * jnp.exp(d - m_t)
    num = jnp.dot(s.astype(BF16), v, preferred_element_type=F32)
    den = jnp.sum(s, axis=1, keepdims=True)
    if has_prev:
        a_inter = jnp.exp(m_inter - m_t)
        num = num + a_inter * jnp.dot(q, c_ref[...].astype(BF16), preferred_element_type=F32)
        den = den + a_inter * jnp.sum(q.astype(F32) * n_ref[...], axis=1, keepdims=True)
    h = num / jnp.maximum(jnp.abs(den), jnp.exp(-m_t))

    b_last = jnp.sum(f_col, axis=0, keepdims=True)
    w_col = b_last - b_col + i_col
    m_new = jnp.maximum(b_last + m_prev, jnp.max(w_col, axis=0, keepdims=True))
    wk = jnp.exp(w_col - m_new) * k.astype(F32)
    c_upd = lax.dot_general(wk.astype(BF16), v, TN_DIMS, preferred_element_type=F32)
    n_upd = jnp.sum(wk, axis=0, keepdims=True)
    if has_prev:
        decay = jnp.exp(b_last + m_prev - m_new)
        c_ref[...] = decay * c_ref[...] + c_upd
        n_ref[...] = decay * n_ref[...] + n_upd
    else:
        c_ref[...] = c_upd
        n_ref[...] = n_upd
    m_ref[...] = m_new
    return h


def _mlstm_kernel(*refs, nc, hps, has_state, emit_state):
    refs = list(refs)
    m0_ref = refs.pop(0) if has_state else None
    qf_ref, kf_ref, vf_ref, gf_ref, gtf_ref = refs[:5]
    refs = refs[5:]
    if nc > 1:
        qb_ref, kb_ref, vb_ref, gb_ref, gtb_ref = refs[:5]
        refs = refs[5:]
    else:
        qb_ref, kb_ref, vb_ref, gb_ref, gtb_ref = qf_ref, kf_ref, vf_ref, gf_ref, gtf_ref
    if has_state:
        c0_ref, n0_ref = refs[:2]
        refs = refs[2:]
    if nc > 1:
        hf_ref, hb_ref = refs[:2]
        refs = refs[2:]
    else:
        hs_ref = refs.pop(0)
    if emit_state:
        cout_ref, nout_ref, mout_ref = refs[:3]
        refs = refs[3:]
    c_sc, n_sc, m_sc = refs

    b = pl.program_id(0)
    head0 = pl.program_id(1) * hps
    c = pl.program_id(2)
    has_prev = has_state or nc > 1
    dk, dv = MLSTM_DK, MLSTM_DV

    if has_prev:
        @pl.when(c == 0)
        def _():
            for dr in range(2):
                for j in range(hps):
                    if has_state:
                        c_sc[dr, j] = c0_ref[0, dr, j]
                        n_sc[dr, j] = n0_ref[0, dr, j]
                        m_sc[dr, j] = jnp.full((1, 1), m0_ref[(b * 2 + dr) * MLSTM_HEADS + head0 + j], F32)
                    else:
                        c_sc[dr, j] = jnp.zeros((dk, dv), F32)
                        n_sc[dr, j] = jnp.zeros((1, dk), F32)
                        m_sc[dr, j] = jnp.zeros((1, 1), F32)

    def gates(g_ref, gt_ref, idx, j):
        g = g_ref[...]
        lane = lax.broadcasted_iota(jnp.int32, g.shape, 1)
        sel = idx * MLSTM_HEADS + head0 + j
        col = jnp.sum(jnp.where(lane == sel, g, 0.0), axis=1, keepdims=True)
        return col, gt_ref[pl.ds(sel, 1), :]

    q_scale = MLSTM_DK ** -0.5
    for j in range(hps):
        outs = []
        for dr, (q_ref, k_ref, v_ref, g_ref, gt_ref) in enumerate(
                ((qf_ref, kf_ref, vf_ref, gf_ref, gtf_ref), (qb_ref, kb_ref, vb_ref, gb_ref, gtb_ref))):
            i_col, i_row = gates(g_ref, gt_ref, 2 * dr, j)
            f_col, f_row = gates(g_ref, gt_ref, 2 * dr + 1, j)
            outs.append(_mlstm_direction(
                q_ref[:, j * dk:(j + 1) * dk] * q_scale, k_ref[:, j * dk:(j + 1) * dk], v_ref[:, j * dv:(j + 1) * dv],
                i_col, f_col, i_row, f_row, c_sc.at[dr, j], n_sc.at[dr, j], m_sc.at[dr, j],
                causal=(dr == 0), has_prev=has_prev))
        if nc > 1:
            hf_ref[:, j * dv:(j + 1) * dv] = outs[0]
            hb_ref[:, j * dv:(j + 1) * dv] = outs[1]
        else:
            hs_ref[:, j * dv:(j + 1) * dv] = outs[0] + outs[1]

    if emit_state:
        @pl.when(c == nc - 1)
        def _():
            for dr in range(2):
                for j in range(hps):
                    cout_ref[0, dr, j] = c_sc[dr, j]
                    nout_ref[0, dr, j] = n_sc[dr, j]
                    mout_ref[0, dr, j] = jnp.broadcast_to(m_sc[dr, j], (1, LANES))


def mlstm_core(u_qkv, g, gt, *, batch, heads_per_step, state=None, emit_state=False):
    m = u_qkv.shape[0]
    L = MLSTM_CHUNK
    nc = m // batch // L
    hps = heads_per_step
    nh, dk, dv = MLSTM_HEADS, hps * MLSTM_DK, hps * MLSTM_DV
    k_blk0 = MLSTM_QK_W // dk
    v_blk0 = 2 * MLSTM_QK_W // dv
    has_state = state is not None

    def specs(chunk_of):
        rb = lambda b, h, c: b * nc + chunk_of(c)
        return [
            pl.BlockSpec((L, dk), lambda b, h, c: (rb(b, h, c), h)),
            pl.BlockSpec((L, dk), lambda b, h, c: (rb(b, h, c), k_blk0 + h)),
            pl.BlockSpec((L, dv), lambda b, h, c: (rb(b, h, c), v_blk0 + h)),
            pl.BlockSpec((L, MLSTM_N_GATES), lambda b, h, c: (rb(b, h, c), 0)),
            pl.BlockSpec((MLSTM_N_GATES, L), lambda b, h, c: (0, rb(b, h, c))),
        ]

    fwd_chunk = lambda c: c
    bwd_chunk = lambda c: nc - 1 - c
    in_specs, args = [], []
    if has_state:
        c0, n0, m0 = state
        in_specs.append(pl.BlockSpec(memory_space=pltpu.SMEM))
        args.append(m0.reshape(-1))
    in_specs += specs(fwd_chunk)
    args += [u_qkv, u_qkv, u_qkv, g, gt]
    if nc > 1:
        in_specs += specs(bwd_chunk)
        args += [u_qkv, u_qkv, u_qkv, g, gt]
    state_specs = [
        pl.BlockSpec((1, 2, hps, MLSTM_DK, MLSTM_DV), lambda b, h, c: (b, 0, h, 0, 0)),
        pl.BlockSpec((1, 2, hps, 1, MLSTM_DK), lambda b, h, c: (b, 0, h, 0, 0)),
    ]
    if has_state:
        in_specs += state_specs
        args += [c0, n0.reshape(batch, 2, nh, 1, MLSTM_DK)]

    h_shape = jax.ShapeDtypeStruct((m, MLSTM_V_W), F32)
    if nc > 1:
        out_specs = [
            pl.BlockSpec((L, dv), lambda b, h, c: (b * nc + fwd_chunk(c), h)),
            pl.BlockSpec((L, dv), lambda b, h, c: (b * nc + bwd_chunk(c), h)),
        ]
        out_shape = [h_shape, h_shape]
    else:
        out_specs = [pl.BlockSpec((L, dv), lambda b, h, c: (b, h))]
        out_shape = [h_shape]
    if emit_state:
        out_specs += state_specs + [pl.BlockSpec((1, 2, hps, 1, LANES), lambda b, h, c: (b, 0, h, 0, 0))]
        out_shape += [
            jax.ShapeDtypeStruct((batch, 2, nh, MLSTM_DK, MLSTM_DV), F32),
            jax.ShapeDtypeStruct((batch, 2, nh, 1, MLSTM_DK), F32),
            jax.ShapeDtypeStruct((batch, 2, nh, 1, LANES), F32),
        ]
    return pl.pallas_call(
        functools.partial(_mlstm_kernel, nc=nc, hps=hps, has_state=has_state, emit_state=emit_state),
        grid=(batch, nh // hps, nc),
        in_specs=in_specs,
        out_specs=out_specs,
        out_shape=out_shape,
        scratch_shapes=[pltpu.VMEM((2, hps, MLSTM_DK, MLSTM_DV), F32), pltpu.VMEM((2, hps, 1, MLSTM_DK), F32),
                        pltpu.VMEM((2, hps, 1, 1), F32)],
        name="mlstm_core",
        compiler_params=_params(("parallel", "parallel", "arbitrary"), VMEM_LIMIT_BIG),
    )(*args)


def _hnorm_kernel(*refs, n_in):
    h_refs, (o_ref, w_ref, out_ref) = refs[:n_in], refs[n_in:]
    for hd in range(MLSTM_HEADS):
        sl = slice(hd * MLSTM_DV, (hd + 1) * MLSTM_DV)
        x = h_refs[0][:, sl]
        for r in h_refs[1:]:
            x = x + r[:, sl]
        y = x * lax.rsqrt(jnp.mean(x * x, axis=-1, keepdims=True) + EPS) * w_ref[:, sl]
        gate = 0.5 * jnp.tanh(0.5 * o_ref[:, sl]) + 0.5
        out_ref[:, sl] = (y * gate).astype(BF16)


def mlstm_hnorm_gate(h_parts, o, hnorm_w):
    m, w = o.shape
    tm = 256
    spec = pl.BlockSpec((tm, w), lambda i: (i, 0))
    return pl.pallas_call(
        functools.partial(_hnorm_kernel, n_in=len(h_parts)),
        grid=(m // tm,),
        in_specs=[spec] * len(h_parts) + [spec, pl.BlockSpec((1, w), lambda i: (0, 0))],
        out_specs=spec,
        out_shape=jax.ShapeDtypeStruct((m, w), BF16),
        name="mlstm_hnorm_gate",
        compiler_params=_params(("parallel",)),
    )(*h_parts, o, hnorm_w.reshape(1, w))


def _rope_tile(y, cos, sin_signed):
    lane = lax.broadcasted_iota(jnp.int32, y.shape, 1)
    partner = jnp.where(lane % ROPE_AXIS_DIM < ROPE_HALF,
                        pltpu.roll(y, LANES - ROPE_HALF, 1), pltpu.roll(y, ROPE_HALF, 1))
    return y * cos + partner * sin_signed


def _mla_lat_kernel(*refs, rope):
    if rope:
        a_ref, qw_ref, kvw_ref, krw_ref, cos_ref, sin_ref, cq_ref, ckv_ref, kr_ref, kr2_ref = refs
    else:
        a_ref, qw_ref, kvw_ref, krw_ref, cq_ref, ckv_ref, kr_ref, kr2_ref = refs
    cq = a_ref[:, :MLA_Q_LORA]
    cq_ref[...] = (cq * lax.rsqrt(jnp.mean(cq * cq, axis=-1, keepdims=True) + EPS) * qw_ref[...]).astype(BF16)
    ckv = a_ref[:, MLA_Q_LORA:MLA_Q_LORA + MLA_KV_LORA]
    ckv_ref[...] = ckv * lax.rsqrt(jnp.mean(ckv * ckv, axis=-1, keepdims=True) + EPS) * kvw_ref[...]
    kr = a_ref[:, MLA_Q_LORA + MLA_KV_LORA:MLA_Q_LORA + MLA_KV_LORA + LANES]
    kr = jnp.where(lax.broadcasted_iota(jnp.int32, kr.shape, 1) < MLA_ROPE, kr, 0.0)
    ms = jnp.sum(kr * kr, axis=-1, keepdims=True) * (1.0 / MLA_ROPE)
    krn = kr * lax.rsqrt(ms + EPS) * krw_ref[...]
    kr_ref[...] = krn
    kr2 = krn + pltpu.roll(krn, MLA_ROPE, 1)
    if rope:
        kr2 = _rope_tile(kr2, cos_ref[...], sin_ref[...])
    kr2_ref[...] = kr2.astype(BF16)


def mla_latents(a, qnorm_w, kvnorm_w, kr_w, rope_tables):
    m = a.shape[0]
    tm = 256
    rope = rope_tables is not None
    in_specs = [
        pl.BlockSpec((tm, MLA_IN_PAD), lambda i: (i, 0)),
        pl.BlockSpec((1, MLA_Q_LORA), lambda i: (0, 0)),
        pl.BlockSpec((1, MLA_KV_LORA), lambda i: (0, 0)),
        pl.BlockSpec((1, LANES), lambda i: (0, 0)),
    ]
    args = [a, qnorm_w.reshape(1, -1), kvnorm_w.reshape(1, -1), kr_w]
    if rope:
        nt = rope_tables[0].shape[0] // tm
        in_specs += [pl.BlockSpec((tm, LANES), lambda i: (i % nt, 0))] * 2
        args += list(rope_tables)
    return pl.pallas_call(
        functools.partial(_mla_lat_kernel, rope=rope),
        grid=(m // tm,),
        in_specs=in_specs,
        out_specs=[
            pl.BlockSpec((tm, MLA_Q_LORA), lambda i: (i, 0)),
            pl.BlockSpec((tm, MLA_KV_LORA), lambda i: (i, 0)),
            pl.BlockSpec((tm, LANES), lambda i: (i, 0)),
            pl.BlockSpec((tm, LANES), lambda i: (i, 0)),
        ],
        out_shape=[
            jax.ShapeDtypeStruct((m, MLA_Q_LORA), BF16),
            jax.ShapeDtypeStruct((m, MLA_KV_LORA), F32),
            jax.ShapeDtypeStruct((m, LANES), F32),
            jax.ShapeDtypeStruct((m, LANES), BF16),
        ],
        name="mla_latents",
        compiler_params=_params(("parallel",)),
    )(*args)


PAIR_W = 2 * MLA_QK_DIM
PAIRS_PER_TILE = 2


def _mla_q_kernel(*refs, rope):
    if rope:
        x_ref, w_ref, nw_ref, rw_ref, cos_ref, sin_ref, o_ref = refs
    else:
        x_ref, w_ref, nw_ref, rw_ref, o_ref = refs
    acc = jnp.dot(x_ref[...], w_ref[...].astype(BF16), preferred_element_type=F32)
    lane = lax.broadcasted_iota(jnp.int32, (acc.shape[0], LANES), 1)
    first = lane < MLA_ROPE
    for p in range(PAIRS_PER_TILE):
        base = p * PAIR_W
        obase = p * 2 * MLA_HEAD_PAD
        t1 = acc[:, base + LANES: base + 2 * LANES]
        t2 = acc[:, base + 2 * LANES: base + PAIR_W]
        for hh, x in enumerate((acc[:, base: base + LANES], jnp.where(first, t2, t1))):
            y = x * lax.rsqrt(jnp.mean(x * x, axis=-1, keepdims=True) + EPS) * nw_ref[hh:hh + 1, :]
            o_ref[:, obase + hh * MLA_HEAD_PAD: obase + hh * MLA_HEAD_PAD + MLA_NOPE] = y.astype(BF16)
        r = jnp.where(first, t1, t2)
        r2 = r * r
        ss_a = jnp.sum(jnp.where(first, r2, 0.0), axis=-1, keepdims=True)
        ss_b = jnp.sum(jnp.where(first, 0.0, r2), axis=-1, keepdims=True)
        inv = jnp.where(first, lax.rsqrt(ss_a * (1.0 / MLA_ROPE) + EPS), lax.rsqrt(ss_b * (1.0 / MLA_ROPE) + EPS))
        rn = r * inv * rw_ref[...]
        if rope:
            rn = _rope_tile(rn, cos_ref[...], sin_ref[...])
        o_ref[:, obase + MLA_NOPE: obase + MLA_HEAD_PAD] = jnp.where(first, rn, 0.0).astype(BF16)
        o_ref[:, obase + MLA_HEAD_PAD + MLA_NOPE: obase + 2 * MLA_HEAD_PAD] = jnp.where(first, 0.0, rn).astype(BF16)


def mla_queries(cqn, w_uq, q_nope_w, q_rope_w2, rope_tables, *, layer):
    m, k = cqn.shape
    tm = 1024
    tn = PAIRS_PER_TILE * PAIR_W
    to = PAIRS_PER_TILE * 2 * MLA_HEAD_PAD
    rope = rope_tables is not None
    in_specs = [
        pl.BlockSpec((tm, k), lambda i, j: (i, 0)),
        pl.BlockSpec((None, k, tn), lambda i, j: (layer, 0, j)),
        pl.BlockSpec((2, MLA_NOPE), lambda i, j: (0, 0)),
        pl.BlockSpec((1, LANES), lambda i, j: (0, 0)),
    ]
    args = [cqn, w_uq, q_nope_w, q_rope_w2]
    if rope:
        nt = rope_tables[0].shape[0] // tm
        in_specs += [pl.BlockSpec((tm, LANES), lambda i, j: (i % nt, 0))] * 2
        args += list(rope_tables)
    return pl.pallas_call(
        functools.partial(_mla_q_kernel, rope=rope),
        grid=(m // tm, w_uq.shape[2] // tn),
        in_specs=in_specs,
        out_specs=pl.BlockSpec((tm, to), lambda i, j: (i, j)),
        out_shape=jax.ShapeDtypeStruct((m, MLA_HEADS * MLA_HEAD_PAD), BF16),
        name="mla_queries",
        compiler_params=_params(("parallel", "parallel")),
    )(*args)


KV_HEADS_PER_TILE = 8


def _mla_kv_kernel(x_ref, w_ref, kr2_ref, nw_ref, k_ref, v_ref):
    pieces = []
    for hh in range(KV_HEADS_PER_TILE):
        base = hh * (MLA_NOPE + MLA_V)
        wk = w_ref[:, base: base + MLA_NOPE]
        pieces += [pltpu.roll(wk, MLA_NOPE // 2, 1) if hh % 2 else wk, w_ref[:, base + MLA_NOPE: base + MLA_NOPE + MLA_V]]
    w = jnp.concatenate(pieces, axis=1).astype(BF16)
    acc = jnp.dot(x_ref[...], w, preferred_element_type=F32)
    kr2 = kr2_ref[...]
    for hh in range(KV_HEADS_PER_TILE):
        base = hh * (MLA_NOPE + MLA_V)
        x = acc[:, base: base + MLA_NOPE]
        y = x * lax.rsqrt(jnp.mean(x * x, axis=-1, keepdims=True) + EPS) * nw_ref[hh % 2:hh % 2 + 1, :]
        k_ref[:, hh * MLA_HEAD_PAD: hh * MLA_HEAD_PAD + MLA_NOPE] = y.astype(BF16)
        k_ref[:, hh * MLA_HEAD_PAD + MLA_NOPE: (hh + 1) * MLA_HEAD_PAD] = kr2
        v_ref[:, hh * MLA_V: (hh + 1) * MLA_V] = acc[:, base + MLA_NOPE: base + MLA_NOPE + MLA_V].astype(BF16)


def mla_keys_values(ckv, kr2, w_ukv, k_nope_w, *, layer, tm):
    m, k = ckv.shape
    tn = KV_HEADS_PER_TILE * (MLA_NOPE + MLA_V)
    return pl.pallas_call(
        _mla_kv_kernel,
        grid=(m // tm, w_ukv.shape[2] // tn),
        in_specs=[
            pl.BlockSpec((tm, k), lambda i, j: (i, 0)),
            pl.BlockSpec((None, k, tn), lambda i, j: (layer, 0, j)),
            pl.BlockSpec((tm, LANES), lambda i, j: (i, 0)),
            pl.BlockSpec((2, MLA_NOPE), lambda i, j: (0, 0)),
        ],
        out_specs=[
            pl.BlockSpec((tm, KV_HEADS_PER_TILE * MLA_HEAD_PAD), lambda i, j: (i, j)),
            pl.BlockSpec((tm, KV_HEADS_PER_TILE * MLA_V), lambda i, j: (i, j)),
        ],
        out_shape=[
            jax.ShapeDtypeStruct((m, MLA_HEADS * MLA_HEAD_PAD), BF16),
            jax.ShapeDtypeStruct((m, MLA_HEADS * MLA_V), BF16),
        ],
        name="mla_keys_values",
        compiler_params=_params(("parallel", "parallel"), VMEM_LIMIT_BIG),
    )(ckv, w_ukv, kr2, k_nope_w)


def _attn_kernel(q_ref, k_ref, v_ref, o_ref, *, heads):
    exp2_scale = float(MLA_QK_DIM ** -0.5 * np.log2(np.e))
    for hh in range(heads):
        q = q_ref[:, hh * MLA_HEAD_PAD:(hh + 1) * MLA_HEAD_PAD]
        k = k_ref[:, hh * MLA_HEAD_PAD:(hh + 1) * MLA_HEAD_PAD]
        s = lax.dot_general(q, k, NT_DIMS, preferred_element_type=F32)
        p = jnp.exp2((s - jnp.max(s, axis=-1, keepdims=True)) * exp2_scale)
        o = jnp.dot(p.astype(BF16), v_ref[:, hh * MLA_V:(hh + 1) * MLA_V], preferred_element_type=F32)
        o_ref[:, hh * MLA_V:(hh + 1) * MLA_V] = (o / jnp.sum(p, axis=-1, keepdims=True)).astype(BF16)


def attention(q, k, v, *, batch, heads_per_step, tq):
    m = q.shape[0]
    t = m // batch
    s = k.shape[0] // batch
    nq = t // tq
    hb = heads_per_step
    return pl.pallas_call(
        functools.partial(_attn_kernel, heads=hb),
        grid=(batch, MLA_HEADS // hb, nq),
        in_specs=[
            pl.BlockSpec((tq, hb * MLA_HEAD_PAD), lambda b, g, i: (b * nq + i, g)),
            pl.BlockSpec((s, hb * MLA_HEAD_PAD), lambda b, g, i: (b, g)),
            pl.BlockSpec((s, hb * MLA_V), lambda b, g, i: (b, g)),
        ],
        out_specs=pl.BlockSpec((tq, hb * MLA_V), lambda b, g, i: (b * nq + i, g)),
        out_shape=jax.ShapeDtypeStruct((m, MLA_HEADS * MLA_V), BF16),
        name="attention",
        compiler_params=_params(("parallel", "parallel", "parallel"), VMEM_LIMIT_BIG),
    )(q, k, v)


def _rope_tables(n_tok):
    pos = np.arange(n_tok)
    row = (pos // GRID_W).astype(np.float32)
    col = (pos % GRID_W).astype(np.float32)
    inv_freq = jnp.asarray(ROPE_BASE, F32) ** (-jnp.arange(0, ROPE_AXIS_DIM, 2, dtype=F32) / ROPE_AXIS_DIM)
    lane = np.arange(LANES)
    freq = inv_freq[lane % ROPE_HALF]
    use_row = (lane % MLA_ROPE) < ROPE_AXIS_DIM
    ang = jnp.where(use_row[None, :], row[:, None], col[:, None]) * freq[None, :]
    sign = np.where(lane % ROPE_AXIS_DIM < ROPE_HALF, -1.0, 1.0).astype(np.float32)
    return jnp.cos(ang), jnp.sin(ang) * sign[None, :]


def _mlstm_layer(h, x, mod, w_in_t, gate_b, hnorm_w, w_out, *, layer, batch, state, emit_state, mod_kw):
    tn = 512
    u_qkv, o = matmul_split(h, w_in_t, layer=layer, n_out1=2 * MLSTM_QK_W + MLSTM_V_W, dtype1=BF16,
                            n_out2=MLSTM_V_W, dtype2=F32, tn=tn)
    g, gt = mlstm_gates(h, w_in_t, gate_b[layer], layer=layer)
    outs = mlstm_core(u_qkv, g, gt, batch=batch, state=state, emit_state=emit_state,
                      heads_per_step=MLSTM_HEADS if state is None else MLSTM_HEADS // 2)
    n_h = len(outs) - (3 if emit_state else 0)
    hg = mlstm_hnorm_gate(outs[:n_h], o, hnorm_w[layer])
    x = matmul(hg, w_out, layer=layer, n_out=D_MODEL, tn=512, out_dtype=F32, epilogue="resid", res=x, mod=mod,
               gate_piece=2, **mod_kw)
    return x, outs[n_h:]


def _mla_layer(h, x, mod, wts, *, layer, batch, ctx, rope_tables, mod_kw):
    w_in_t, qnorm_w, kvnorm_w, w_uq, w_ukv, q_nope_w, q_rope_w2, k_nope_w, kr_w, w_out = wts
    a = matmul(h, w_in_t, layer=layer, w_is_nk=True, n_out=MLA_IN_PAD, tn=512, out_dtype=F32)
    cqn, ckvn, krn, kr2 = mla_latents(a, qnorm_w, kvnorm_w, kr_w, rope_tables)
    q = mla_queries(cqn, w_uq, q_nope_w, q_rope_w2, rope_tables, layer=layer)
    ckv_all, kr2_all = ckvn.astype(BF16), kr2
    if ctx is not None:
        ckv_ctx, kr_ctx = ctx
        t = ckvn.shape[0] // batch
        kr2_ctx = jnp.concatenate([kr_ctx, kr_ctx], axis=-1).astype(BF16)
        ckv_all = jnp.concatenate([ckv_ctx.astype(BF16), ckv_all.reshape(batch, t, -1)], axis=1).reshape(-1, MLA_KV_LORA)
        kr2_all = jnp.concatenate([kr2_ctx, kr2.reshape(batch, t, -1)], axis=1).reshape(-1, LANES)
    k, v = mla_keys_values(ckv_all, kr2_all, w_ukv, k_nope_w, layer=layer, tm=1024 if ctx is None else 1536)
    if ctx is None:
        o = attention(q, k, v, batch=batch, heads_per_step=8, tq=q.shape[0] // batch)
    else:
        o = attention(q, k, v, batch=batch, heads_per_step=8, tq=512)
    x = matmul(o, w_out, layer=layer, n_out=D_MODEL, tn=512, out_dtype=F32, epilogue="resid", res=x, mod=mod,
               gate_piece=2, **mod_kw)
    return x, ckvn, krn


def _mlp(x, mod, norm_w, w1, w2, *, layer, mod_kw):
    h = norm_modulate(x, norm_w[layer], mod, 3, **mod_kw)
    u = matmul(h, w1, layer=layer, n_out=D_FF, tn=512, out_dtype=BF16, epilogue="relu2")
    return matmul_ksplit_resid(u, w2, x, mod, layer=layer, gate_piece=5, **mod_kw)


def kernel(x_prompt, x_sample, state_mlstm_C, state_mlstm_n, state_mlstm_m, cache_mla_ckv, cache_mla_krope, c, c_ctx,
           norm1_w, norm2_w, mod_w, mod_b, mlp_w1, mlp_w2,
           mlstm_w_in, mlstm_gate_b, mlstm_hnorm_w, mlstm_w_out,
           mla_w_in, mla_qnorm_w, mla_kvnorm_w, mla_w_uq, mla_w_ukv, mla_q_norm_w, mla_k_norm_w, mla_w_out):
    bp, tp, d = x_prompt.shape
    bs, ts, _ = x_sample.shape
    xp = x_prompt.reshape(bp * tp, d)
    xs = x_sample.reshape(bs * ts, d)
    cond = jnp.concatenate([c_ctx[None, :], c, jnp.zeros((MOD_ROWS - 1 - bs, d), F32)], axis=0)
    mod = modulation_table(cond, mod_w, mod_b)
    kw_p = dict(rows_per_mod=bp * tp, mod_base=0)
    kw_s = dict(rows_per_mod=ts, mod_base=1)

    wa = (jnp.swapaxes(mlstm_w_in, 1, 2), mlstm_gate_b, mlstm_hnorm_w, mlstm_w_out)
    hp = norm_modulate(xp, norm1_w[0], mod[0], 0, **kw_p)
    hs = norm_modulate(xs, norm1_w[0], mod[0], 0, **kw_s)
    xp, (c_new, n_new, m_new) = _mlstm_layer(hp, xp, mod[0], *wa, layer=0, batch=bp, state=None, emit_state=True,
                                             mod_kw=kw_p)
    state = (state_mlstm_C[:, 0], state_mlstm_n[:, 0], state_mlstm_m[:, 0])
    xs, _ = _mlstm_layer(hs, xs, mod[0], *wa, layer=0, batch=bs, state=state, emit_state=False, mod_kw=kw_s)
    xp = _mlp(xp, mod[0], norm2_w, mlp_w1, mlp_w2, layer=0, mod_kw=kw_p)
    xs = _mlp(xs, mod[0], norm2_w, mlp_w1, mlp_w2, layer=0, mod_kw=kw_s)

    qw, kw = mla_q_norm_w[0], mla_k_norm_w[0]
    pad = lambda w: jnp.concatenate([w, jnp.zeros_like(w)]).reshape(1, LANES)
    both_orders = lambda w: jnp.stack([w, jnp.roll(w, MLA_NOPE // 2)])
    wb = (
        jnp.swapaxes(mla_w_in, 1, 2), mla_qnorm_w[0], mla_kvnorm_w[0], mla_w_uq, mla_w_ukv,
        both_orders(qw[:MLA_NOPE]), jnp.concatenate([qw[MLA_NOPE:], qw[MLA_NOPE:]]).reshape(1, LANES),
        both_orders(kw[:MLA_NOPE]), pad(kw[MLA_NOPE:]), mla_w_out,
    )
    hp = norm_modulate(xp, norm1_w[1], mod[1], 0, **kw_p)
    hs = norm_modulate(xs, norm1_w[1], mod[1], 0, **kw_s)
    xp, ckv_p, kr_p = _mla_layer(hp, xp, mod[1], wb, layer=0, batch=bp, ctx=None, rope_tables=None, mod_kw=kw_p)
    xs, _, _ = _mla_layer(hs, xs, mod[1], wb, layer=0, batch=bs, ctx=(cache_mla_ckv[:, 0], cache_mla_krope[:, 0]),
                          rope_tables=_rope_tables(ts), mod_kw=kw_s)
    xp = _mlp(xp, mod[1], norm2_w, mlp_w1, mlp_w2, layer=1, mod_kw=kw_p)
    xs = _mlp(xs, mod[1], norm2_w, mlp_w1, mlp_w2, layer=1, mod_kw=kw_s)

    return (
        xp.reshape(bp, tp, d),
        xs.reshape(bs, ts, d),
        c_new.reshape(bp, 1, 2, MLSTM_HEADS, MLSTM_DK, MLSTM_DV),
        n_new.reshape(bp, 1, 2, MLSTM_HEADS, MLSTM_DK),
        m_new[..., 0, 0].reshape(bp, 1, 2, MLSTM_HEADS),
        ckv_p.reshape(bp, 1, tp, MLA_KV_LORA),
        kr_p[:, :MLA_ROPE].reshape(bp, 1, tp, MLA_ROPE),
    )
```

```python
import functools

import jax
import jax.numpy as jnp
import numpy as np
from jax import lax
from jax.experimental import pallas as pl
from jax.experimental.pallas import tpu as pltpu

F32 = jnp.float32
BF16 = jnp.bfloat16

D_MODEL = 4096
D_FF = 4 * D_MODEL
N_MOD = 6
EPS = 1e-6
GRID_W = 64

MLSTM_HEADS = 8
MLSTM_DK = 256
MLSTM_DV = 512
MLSTM_QK_W = MLSTM_HEADS * MLSTM_DK
MLSTM_V_W = MLSTM_HEADS * MLSTM_DV
MLSTM_N_GATES = 4 * MLSTM_HEADS
MLSTM_CHUNK = 256

MLA_HEADS = 32
MLA_NOPE = 128
MLA_ROPE = 64
MLA_V = 128
MLA_QK_DIM = MLA_NOPE + MLA_ROPE
MLA_Q_LORA = 1024
MLA_KV_LORA = 512
MLA_IN_PAD = 2048
MLA_HEAD_PAD = 256
ROPE_AXIS_DIM = MLA_ROPE // 2
ROPE_HALF = ROPE_AXIS_DIM // 2
ROPE_BASE = 10000.0

LANES = 128
SUBLANES = 8
MOD_ROWS = SUBLANES
VMEM_LIMIT_BIG = 58 * 1024 * 1024
VMEM_LIMIT_MED = 40 * 1024 * 1024

NT_DIMS = (((1,), (1,)), ((), ()))
TN_DIMS = (((0,), (0,)), ((), ()))


def _params(semantics, vmem=VMEM_LIMIT_MED):
    return pltpu.CompilerParams(dimension_semantics=semantics, vmem_limit_bytes=vmem)


def _mod_row(tile, tm, rows_per_mod, mod_base):
    return mod_base + (tile * tm) // rows_per_mod


def _mod_kernel(c_ref, w_ref, b_ref, o_ref):
    c = c_ref[...]
    a = (c * (1.0 / (1.0 + jnp.exp(-c)))).astype(BF16)
    o_ref[0] = jnp.dot(a, w_ref[0].astype(BF16), preferred_element_type=F32) + b_ref[0]


def modulation_table(cond, mod_w, mod_b):
    depth, d, n = mod_w.shape
    tn = 1024
    return pl.pallas_call(
        _mod_kernel,
        grid=(depth, n // tn),
        in_specs=[
            pl.BlockSpec((MOD_ROWS, d), lambda l, j: (0, 0)),
            pl.BlockSpec((1, d, tn), lambda l, j: (l, 0, j)),
            pl.BlockSpec((1, 1, tn), lambda l, j: (l, 0, j)),
        ],
        out_specs=pl.BlockSpec((1, MOD_ROWS, tn), lambda l, j: (l, 0, j)),
        out_shape=jax.ShapeDtypeStruct((depth, MOD_ROWS, n), F32),
        name="modulation_table",
        compiler_params=_params(("parallel", "parallel"), VMEM_LIMIT_BIG),
    )(cond, mod_w, mod_b.reshape(depth, 1, n))


def _normmod_kernel(x_ref, nw_ref, shift_ref, scale_ref, o_ref, *, tm, rows_per_mod, mod_base):
    x = x_ref[...]
    row = _mod_row(pl.program_id(0), tm, rows_per_mod, mod_base)
    shift = shift_ref[pl.ds(row, 1), :]
    gain = nw_ref[...] * (1.0 + scale_ref[pl.ds(row, 1), :])
    y = x * lax.rsqrt(jnp.mean(x * x, axis=-1, keepdims=True) + EPS)
    o_ref[...] = (y * gain + shift).astype(BF16)


def norm_modulate(x, norm_w, mod, piece, *, rows_per_mod, mod_base):
    m, d = x.shape
    tm = 512
    return pl.pallas_call(
        functools.partial(_normmod_kernel, tm=tm, rows_per_mod=rows_per_mod, mod_base=mod_base),
        grid=(m // tm,),
        in_specs=[
            pl.BlockSpec((tm, d), lambda i: (i, 0)),
            pl.BlockSpec((1, d), lambda i: (0, 0)),
            pl.BlockSpec((MOD_ROWS, d), lambda i: (0, piece)),
            pl.BlockSpec((MOD_ROWS, d), lambda i: (0, piece + 1)),
        ],
        out_specs=pl.BlockSpec((tm, d), lambda i: (i, 0)),
        out_shape=jax.ShapeDtypeStruct((m, d), BF16),
        name="norm_modulate",
        compiler_params=_params(("parallel",)),
    )(x, norm_w.reshape(1, d), mod, mod)


def _mm_kernel(*refs, epilogue, w_is_nk, n_valid, tm, rows_per_mod, mod_base):
    if epilogue == "resid":
        x_ref, w_ref, res_ref, gate_ref, o_ref = refs
    else:
        x_ref, w_ref, o_ref = refs
    if w_is_nk:
        acc = lax.dot_general(x_ref[...], w_ref[...].astype(BF16), NT_DIMS, preferred_element_type=F32)
    else:
        acc = jnp.dot(x_ref[...], w_ref[...].astype(BF16), preferred_element_type=F32)
    if n_valid is not None:
        col = pl.program_id(1) * acc.shape[1] + lax.broadcasted_iota(jnp.int32, acc.shape, 1)
        acc = jnp.where(col < n_valid, acc, 0.0)
    if epilogue == "relu2":
        r = jnp.maximum(acc, 0.0)
        o_ref[...] = (r * r).astype(o_ref.dtype)
    elif epilogue == "resid":
        row = _mod_row(pl.program_id(0), tm, rows_per_mod, mod_base)
        o_ref[...] = res_ref[...] + gate_ref[pl.ds(row, 1), :] * acc
    else:
        o_ref[...] = acc.astype(o_ref.dtype)


def matmul(x, w, *, n_out, tn, out_dtype, layer=0, col_off=0, w_is_nk=False, epilogue=None, res=None, mod=None,
           gate_piece=0, rows_per_mod=1, mod_base=0, tm=2048, x_buffers=1):
    m, k = x.shape
    n_w = w.shape[1] if w_is_nk else w.shape[2]
    n_valid = n_w - col_off * tn if n_w - col_off * tn < n_out else None
    if w_is_nk:
        w_spec = pl.BlockSpec((None, tn, k), lambda i, j: (layer, j + col_off, 0))
    else:
        w_spec = pl.BlockSpec((None, k, tn), lambda i, j: (layer, 0, j + col_off))
    in_specs = [pl.BlockSpec((tm, k), lambda i, j: (i, 0), pipeline_mode=pl.Buffered(x_buffers)), w_spec]
    args = [x, w]
    if epilogue == "resid":
        gate_off = gate_piece * (D_MODEL // tn)
        in_specs += [
            pl.BlockSpec((tm, tn), lambda i, j: (i, j)),
            pl.BlockSpec((MOD_ROWS, tn), lambda i, j: (0, gate_off + j)),
        ]
        args += [res, mod]
    return pl.pallas_call(
        functools.partial(_mm_kernel, epilogue=epilogue, w_is_nk=w_is_nk, n_valid=n_valid, tm=tm,
                          rows_per_mod=rows_per_mod, mod_base=mod_base),
        grid=(m // tm, n_out // tn),
        in_specs=in_specs,
        out_specs=pl.BlockSpec((tm, tn), lambda i, j: (i, j)),
        out_shape=jax.ShapeDtypeStruct((m, n_out), out_dtype),
        name=f"matmul_{epilogue or 'plain'}",
        compiler_params=_params(("parallel", "parallel"), VMEM_LIMIT_BIG),
    )(*args)


def _mm_split_kernel(x_ref, w_ref, o1_ref, o2_ref, *, n1):
    j = pl.program_id(1)

    def prod():
        return lax.dot_general(x_ref[...], w_ref[...].astype(BF16), NT_DIMS, preferred_element_type=F32)

    @pl.when(j < n1)
    def _():
        o1_ref[...] = prod().astype(o1_ref.dtype)

    @pl.when(j >= n1)
    def _():
        o2_ref[...] = prod().astype(o2_ref.dtype)


def matmul_split(x, w_t, *, layer, n_out1, dtype1, n_out2, dtype2, tn, tm=2048):
    m, k = x.shape
    n1, n2 = n_out1 // tn, n_out2 // tn
    return pl.pallas_call(
        functools.partial(_mm_split_kernel, n1=n1),
        grid=(m // tm, n1 + n2),
        in_specs=[
            pl.BlockSpec((tm, k), lambda i, j: (i, 0), pipeline_mode=pl.Buffered(1)),
            pl.BlockSpec((None, tn, k), lambda i, j: (layer, j, 0)),
        ],
        out_specs=[
            pl.BlockSpec((tm, tn), lambda i, j: (i, jnp.minimum(j, n1 - 1))),
            pl.BlockSpec((tm, tn), lambda i, j: (i, jnp.maximum(j - n1, 0))),
        ],
        out_shape=[jax.ShapeDtypeStruct((m, n_out1), dtype1), jax.ShapeDtypeStruct((m, n_out2), dtype2)],
        name="matmul_split",
        compiler_params=_params(("parallel", "arbitrary"), VMEM_LIMIT_BIG),
    )(x, w_t)


def _mm_ksplit_kernel(x_ref, w_ref, res_ref, gate_ref, o_ref, *, nk, tm, rows_per_mod, mod_base):
    k = pl.program_id(2)
    row = _mod_row(pl.program_id(0), tm, rows_per_mod, mod_base)

    def part():
        return gate_ref[pl.ds(row, 1), :] * jnp.dot(x_ref[...], w_ref[...].astype(BF16),
                                                    preferred_element_type=F32)

    @pl.when(k == 0)
    def _():
        o_ref[...] = part()

    @pl.when(k > 0)
    def _():
        o_ref[...] = o_ref[...] + part()

    slab = tm // nk
    rows = pl.ds(pl.multiple_of(k * slab, slab), slab)
    o_ref[rows, :] = o_ref[rows, :] + res_ref[...]


def matmul_ksplit_resid(x, w, res, mod, *, layer, gate_piece, rows_per_mod, mod_base, tm=2048, tn=1024, tk=2048):
    m, kdim = x.shape
    n = w.shape[2]
    nk = kdim // tk
    gate_off = gate_piece * (D_MODEL // tn)
    return pl.pallas_call(
        functools.partial(_mm_ksplit_kernel, nk=nk, tm=tm, rows_per_mod=rows_per_mod, mod_base=mod_base),
        grid=(m // tm, n // tn, nk),
        in_specs=[
            pl.BlockSpec((tm, tk), lambda i, j, k: (i, k)),
            pl.BlockSpec((None, tk, tn), lambda i, j, k: (layer, k, j)),
            pl.BlockSpec((tm // nk, tn), lambda i, j, k: (i * nk + k, j)),
            pl.BlockSpec((MOD_ROWS, tn), lambda i, j, k: (0, gate_off + j)),
        ],
        out_specs=pl.BlockSpec((tm, tn), lambda i, j, k: (i, j), pipeline_mode=pl.Buffered(1)),
        out_shape=jax.ShapeDtypeStruct((m, n), F32),
        name="matmul_ksplit_resid",
        compiler_params=_params(("parallel", "parallel", "arbitrary"), VMEM_LIMIT_BIG),
    )(x, w, res, mod)


def _log_sigmoid(x):
    return jnp.minimum(x, 0.0) - jnp.log(1.0 + jnp.exp(-jnp.abs(x)))


def _gate_kernel(h_ref, wg_ref, b_ref, g_ref, gt_ref):
    gt = lax.dot_general(wg_ref[...].astype(BF16), h_ref[...], NT_DIMS, preferred_element_type=F32)
    row = lax.broadcasted_iota(jnp.int32, gt.shape, 0)
    gt = jnp.where(row < MLSTM_N_GATES, gt, 0.0) + b_ref[...]
    gt = jnp.where((row // MLSTM_HEADS) % 2 == 1, _log_sigmoid(gt), gt)
    gt_ref[...] = gt[:MLSTM_N_GATES, :]
    g_ref[...] = gt.T[:, :MLSTM_N_GATES]


def mlstm_gates(h, w_in_t, gate_b, *, layer):
    m, d = h.shape
    ng = MLSTM_N_GATES
    tm = 1024
    gate_blk = (2 * MLSTM_QK_W + 2 * MLSTM_V_W) // LANES
    bias = jnp.zeros((LANES, 1), F32).at[:ng, 0].set(gate_b)
    return pl.pallas_call(
        _gate_kernel,
        grid=(m // tm,),
        in_specs=[
            pl.BlockSpec((tm, d), lambda i: (i, 0)),
            pl.BlockSpec((None, LANES, d), lambda i: (layer, gate_blk, 0)),
            pl.BlockSpec((LANES, 1), lambda i: (0, 0)),
        ],
        out_specs=[pl.BlockSpec((tm, ng), lambda i: (i, 0)), pl.BlockSpec((ng, tm), lambda i: (0, i))],
        out_shape=[jax.ShapeDtypeStruct((m, ng), F32), jax.ShapeDtypeStruct((ng, m), F32)],
        name="mlstm_gates",
        compiler_params=_params(("parallel",)),
    )(h, w_in_t, bias)


def _mlstm_direction(q, k, v, i_col, f_col, i_row, f_row, c_ref, n_ref, m_ref, *, causal, has_prev):
    L = q.shape[0]
    t_idx = lax.broadcasted_iota(jnp.int32, (L, L), 0)
    s_idx = lax.broadcasted_iota(jnp.int32, (L, L), 1)
    if causal:
        mask, mask_t = s_idx <= t_idx, t_idx <= s_idx
    else:
        mask, mask_t = s_idx >= t_idx, t_idx >= s_idx
    b_col = jnp.sum(jnp.where(mask, f_row, 0.0), axis=1, keepdims=True)
    b_row = jnp.sum(jnp.where(mask_t, f_col, 0.0), axis=0, keepdims=True)
    d = jnp.where(mask, b_col - b_row + i_row, -jnp.inf)
    m_prev = m_ref[...] if has_prev else jnp.zeros((1, 1), F32)
    m_inter = b_col + m_prev
    m_t = jnp.maximum(m_inter, jnp.max(d, axis=1, keepdims=True))
    s = lax.dot_general(q, k, NT_DIMS, preferred_element_type=F32) * jnp.exp(d - m_t)
    num = jnp.dot(s.astype(BF16), v, preferred_element_type=F32)
    den = jnp.sum(s, axis=1, keepdims=True)
    if has_prev:
        a_inter = jnp.exp(m_inter - m_t)
        num = num + a_inter * jnp.dot(q, c_ref[...].astype(BF16), preferred_element_type=F32)
        den = den + a_inter * jnp.sum(q.astype(F32) * n_ref[...], axis=1, keepdims=True)
    h = num / jnp.maximum(jnp.abs(den), jnp.exp(-m_t))

    b_last = jnp.sum(f_col, axis=0, keepdims=True)
    w_col = b_last - b_col + i_col
    m_new = jnp.maximum(b_last + m_prev, jnp.max(w_col, axis=0, keepdims=True))
    wk = jnp.exp(w_col - m_new) * k.astype(F32)
    c_upd = lax.dot_general(wk.astype(BF16), v, TN_DIMS, preferred_element_type=F32)
    n_upd = jnp.sum(wk, axis=0, keepdims=True)
    if has_prev:
        decay = jnp.exp(b_last + m_prev - m_new)
        c_ref[...] = decay * c_ref[...] + c_upd
        n_ref[...] = decay * n_ref[...] + n_upd
    else:
        c_ref[...] = c_upd
        n_ref[...] = n_upd
    m_ref[...] = m_new
    return h


def _mlstm_kernel(*refs, nc, hps, has_state, emit_state):
    refs = list(refs)
    m0_ref = refs.pop(0) if has_state else None
    qf_ref, kf_ref, vf_ref, gf_ref, gtf_ref = refs[:5]
    refs = refs[5:]
    if nc > 1:
        qb_ref, kb_ref, vb_ref, gb_ref, gtb_ref = refs[:5]
        refs = refs[5:]
    else:
        qb_ref, kb_ref, vb_ref, gb_ref, gtb_ref = qf_ref, kf_ref, vf_ref, gf_ref, gtf_ref
    if has_state:
        c0_ref, n0_ref = refs[:2]
        refs = refs[2:]
    if nc > 1:
        hf_ref, hb_ref = refs[:2]
        refs = refs[2:]
    else:
        hs_ref = refs.pop(0)
    if emit_state:
        cout_ref, nout_ref, mout_ref = refs[:3]
        refs = refs[3:]
    c_sc, n_sc, m_sc = refs

    b = pl.program_id(0)
    head0 = pl.program_id(1) * hps
    c = pl.program_id(2)
    has_prev = has_state or nc > 1
    dk, dv = MLSTM_DK, MLSTM_DV

    if has_prev:
        @pl.when(c == 0)
        def _():
            for dr in range(2):
                for j in range(hps):
                    if has_state:
                        c_sc[dr, j] = c0_ref[0, dr, j]
                        n_sc[dr, j] = n0_ref[0, dr, j]
                        m_sc[dr, j] = jnp.full((1, 1), m0_ref[(b * 2 + dr) * MLSTM_HEADS + head0 + j], F32)
                    else:
                        c_sc[dr, j] = jnp.zeros((dk, dv), F32)
                        n_sc[dr, j] = jnp.zeros((1, dk), F32)
                        m_sc[dr, j] = jnp.zeros((1, 1), F32)

    def gates(g_ref, gt_ref, idx, j):
        g = g_ref[...]
        lane = lax.broadcasted_iota(jnp.int32, g.shape, 1)
        sel = idx * MLSTM_HEADS + head0 + j
        col = jnp.sum(jnp.where(lane == sel, g, 0.0), axis=1, keepdims=True)
        return col, gt_ref[pl.ds(sel, 1), :]

    q_scale = MLSTM_DK ** -0.5
    for j in range(hps):
        outs = []
        for dr, (q_ref, k_ref, v_ref, g_ref, gt_ref) in enumerate(
                ((qf_ref, kf_ref, vf_ref, gf_ref, gtf_ref), (qb_ref, kb_ref, vb_ref, gb_ref, gtb_ref))):
            i_col, i_row = gates(g_ref, gt_ref, 2 * dr, j)
            f_col, f_row = gates(g_ref, gt_ref, 2 * dr + 1, j)
            outs.append(_mlstm_direction(
                q_ref[:, j * dk:(j + 1) * dk] * q_scale, k_ref[:, j * dk:(j + 1) * dk], v_ref[:, j * dv:(j + 1) * dv],
                i_col, f_col, i_row, f_row, c_sc.at[dr, j], n_sc.at[dr, j], m_sc.at[dr, j],
                causal=(dr == 0), has_prev=has_prev))
        if nc > 1:
            hf_ref[:, j * dv:(j + 1) * dv] = outs[0]
            hb_ref[:, j * dv:(j + 1) * dv] = outs[1]
        else:
            hs_ref[:, j * dv:(j + 1) * dv] = outs[0] + outs[1]

    if emit_state:
        @pl.when(c == nc - 1)
        def _():
            for dr in range(2):
                for j in range(hps):
                    cout_ref[0, dr, j] = c_sc[dr, j]
                    nout_ref[0, dr, j] = n_sc[dr, j]
                    mout_ref[0, dr, j] = jnp.broadcast_to(m_sc[dr, j], (1, LANES))


def mlstm_core(u_qkv, g, gt, *, batch, heads_per_step, state=None, emit_state=False):
    m = u_qkv.shape[0]
    L = MLSTM_CHUNK
    nc = m // batch // L
    hps = heads_per_step
    nh, dk, dv = MLSTM_HEADS, hps * MLSTM_DK, hps * MLSTM_DV
    k_blk0 = MLSTM_QK_W // dk
    v_blk0 = 2 * MLSTM_QK_W // dv
    has_state = state is not None

    def specs(chunk_of):
        rb = lambda b, h, c: b * nc + chunk_of(c)
        return [
            pl.BlockSpec((L, dk), lambda b, h, c: (rb(b, h, c), h)),
            pl.BlockSpec((L, dk), lambda b, h, c: (rb(b, h, c), k_blk0 + h)),
            pl.BlockSpec((L, dv), lambda b, h, c: (rb(b, h, c), v_blk0 + h)),
            pl.BlockSpec((L, MLSTM_N_GATES), lambda b, h, c: (rb(b, h, c), 0)),
            pl.BlockSpec((MLSTM_N_GATES, L), lambda b, h, c: (0, rb(b, h, c))),
        ]

    fwd_chunk = lambda c: c
    bwd_chunk = lambda c: nc - 1 - c
    in_specs, args = [], []
    if has_state:
        c0, n0, m0 = state
        in_specs.append(pl.BlockSpec(memory_space=pltpu.SMEM))
        args.append(m0.reshape(-1))
    in_specs += specs(fwd_chunk)
    args += [u_qkv, u_qkv, u_qkv, g, gt]
    if nc > 1:
        in_specs += specs(bwd_chunk)
        args += [u_qkv, u_qkv, u_qkv, g, gt]
    state_specs = [
        pl.BlockSpec((1, 2, hps, MLSTM_DK, MLSTM_DV), lambda b, h, c: (b, 0, h, 0, 0)),
        pl.BlockSpec((1, 2, hps, 1, MLSTM_DK), lambda b, h, c: (b, 0, h, 0, 0)),
    ]
    if has_state:
        in_specs += state_specs
        args += [c0, n0.reshape(batch, 2, nh, 1, MLSTM_DK)]

    h_shape = jax.ShapeDtypeStruct((m, MLSTM_V_W), F32)
    if nc > 1:
        out_specs = [
            pl.BlockSpec((L, dv), lambda b, h, c: (b * nc + fwd_chunk(c), h)),
            pl.BlockSpec((L, dv), lambda b, h, c: (b * nc + bwd_chunk(c), h)),
        ]
        out_shape = [h_shape, h_shape]
    else:
        out_specs = [pl.BlockSpec((L, dv), lambda b, h, c: (b, h))]
        out_shape = [h_shape]
    if emit_state:
        out_specs += state_specs + [pl.BlockSpec((1, 2, hps, 1, LANES), lambda b, h, c: (b, 0, h, 0, 0))]
        out_shape += [
            jax.ShapeDtypeStruct((batch, 2, nh, MLSTM_DK, MLSTM_DV), F32),
            jax.ShapeDtypeStruct((batch, 2, nh, 1, MLSTM_DK), F32),
            jax.ShapeDtypeStruct((batch, 2, nh, 1, LANES), F32),
        ]
    return pl.pallas_call(
        functools.partial(_mlstm_kernel, nc=nc, hps=hps, has_state=has_state, emit_state=emit_state),
        grid=(batch, nh // hps, nc),
        in_specs=in_specs,
        out_specs=out_specs,
        out_shape=out_shape,
        scratch_shapes=[pltpu.VMEM((2, hps, MLSTM_DK, MLSTM_DV), F32), pltpu.VMEM((2, hps, 1, MLSTM_DK), F32),
                        pltpu.VMEM((2, hps, 1, 1), F32)],
        name="mlstm_core",
        compiler_params=_params(("parallel", "parallel", "arbitrary"), VMEM_LIMIT_BIG),
    )(*args)


def _hnorm_kernel(*refs, n_in):
    h_refs, (o_ref, w_ref, out_ref) = refs[:n_in], refs[n_in:]
    for hd in range(MLSTM_HEADS):
        sl = slice(hd * MLSTM_DV, (hd + 1) * MLSTM_DV)
        x = h_refs[0][:, sl]
        for r in h_refs[1:]:
            x = x + r[:, sl]
        y = x * lax.rsqrt(jnp.mean(x * x, axis=-1, keepdims=True) + EPS) * w_ref[:, sl]
        gate = 0.5 * jnp.tanh(0.5 * o_ref[:, sl]) + 0.5
        out_ref[:, sl] = (y * gate).astype(BF16)


def mlstm_hnorm_gate(h_parts, o, hnorm_w):
    m, w = o.shape
    tm = 256
    spec = pl.BlockSpec((tm, w), lambda i: (i, 0))
    return pl.pallas_call(
        functools.partial(_hnorm_kernel, n_in=len(h_parts)),
        grid=(m // tm,),
        in_specs=[spec] * len(h_parts) + [spec, pl.BlockSpec((1, w), lambda i: (0, 0))],
        out_specs=spec,
        out_shape=jax.ShapeDtypeStruct((m, w), BF16),
        name="mlstm_hnorm_gate",
        compiler_params=_params(("parallel",)),
    )(*h_parts, o, hnorm_w.reshape(1, w))


def _rope_tile(y, cos, sin_signed):
    lane = lax.broadcasted_iota(jnp.int32, y.shape, 1)
    partner = jnp.where(lane % ROPE_AXIS_DIM < ROPE_HALF,
                        pltpu.roll(y, LANES - ROPE_HALF, 1), pltpu.roll(y, ROPE_HALF, 1))
    return y * cos + partner * sin_signed


def _mla_lat_kernel(*refs, rope):
    if rope:
        a_ref, qw_ref, kvw_ref, krw_ref, cos_ref, sin_ref, cq_ref, ckv_ref, kr_ref, kr2_ref = refs
    else:
        a_ref, qw_ref, kvw_ref, krw_ref, cq_ref, ckv_ref, kr_ref, kr2_ref = refs
    cq = a_ref[:, :MLA_Q_LORA]
    cq_ref[...] = (cq * lax.rsqrt(jnp.mean(cq * cq, axis=-1, keepdims=True) + EPS) * qw_ref[...]).astype(BF16)
    ckv = a_ref[:, MLA_Q_LORA:MLA_Q_LORA + MLA_KV_LORA]
    ckv_ref[...] = ckv * lax.rsqrt(jnp.mean(ckv * ckv, axis=-1, keepdims=True) + EPS) * kvw_ref[...]
    kr = a_ref[:, MLA_Q_LORA + MLA_KV_LORA:MLA_Q_LORA + MLA_KV_LORA + LANES]
    ms = jnp.sum(kr * kr, axis=-1, keepdims=True) * (1.0 / MLA_ROPE)
    krn = kr * lax.rsqrt(ms + EPS) * krw_ref[...]
    kr_ref[...] = krn
    kr2 = krn + pltpu.roll(krn, MLA_ROPE, 1)
    if rope:
        kr2 = _rope_tile(kr2, cos_ref[...], sin_ref[...])
    kr2_ref[...] = kr2.astype(BF16)


def mla_latents(a, qnorm_w, kvnorm_w, kr_w, rope_tables):
    m = a.shape[0]
    tm = 256
    rope = rope_tables is not None
    in_specs = [
        pl.BlockSpec((tm, MLA_IN_PAD), lambda i: (i, 0)),
        pl.BlockSpec((1, MLA_Q_LORA), lambda i: (0, 0)),
        pl.BlockSpec((1, MLA_KV_LORA), lambda i: (0, 0)),
        pl.BlockSpec((1, LANES), lambda i: (0, 0)),
    ]
    args = [a, qnorm_w.reshape(1, -1), kvnorm_w.reshape(1, -1), kr_w]
    if rope:
        nt = rope_tables[0].shape[0] // tm
        in_specs += [pl.BlockSpec((tm, LANES), lambda i: (i % nt, 0))] * 2
        args += list(rope_tables)
    return pl.pallas_call(
        functools.partial(_mla_lat_kernel, rope=rope),
        grid=(m // tm,),
        in_specs=in_specs,
        out_specs=[
            pl.BlockSpec((tm, MLA_Q_LORA), lambda i: (i, 0)),
            pl.BlockSpec((tm, MLA_KV_LORA), lambda i: (i, 0)),
            pl.BlockSpec((tm, LANES), lambda i: (i, 0)),
            pl.BlockSpec((tm, LANES), lambda i: (i, 0)),
        ],
        out_shape=[
            jax.ShapeDtypeStruct((m, MLA_Q_LORA), BF16),
            jax.ShapeDtypeStruct((m, MLA_KV_LORA), F32),
            jax.ShapeDtypeStruct((m, LANES), F32),
            jax.ShapeDtypeStruct((m, LANES), BF16),
        ],
        name="mla_latents",
        compiler_params=_params(("parallel",)),
    )(*args)


PAIR_W = 2 * MLA_QK_DIM
PAIRS_PER_TILE = 2


def _mla_q_kernel(*refs, rope):
    if rope:
        x_ref, w_ref, nw_ref, rw_ref, cos_ref, sin_ref, o_ref = refs
    else:
        x_ref, w_ref, nw_ref, rw_ref, o_ref = refs
    acc = jnp.dot(x_ref[...], w_ref[...].astype(BF16), preferred_element_type=F32)
    lane = lax.broadcasted_iota(jnp.int32, (acc.shape[0], LANES), 1)
    first = lane < MLA_ROPE
    for p in range(PAIRS_PER_TILE):
        base = p * PAIR_W
        obase = p * 2 * MLA_HEAD_PAD
        t1 = acc[:, base + LANES: base + 2 * LANES]
        t2 = acc[:, base + 2 * LANES: base + PAIR_W]
        for hh, x in enumerate((acc[:, base: base + LANES], jnp.where(first, t2, t1))):
            y = x * lax.rsqrt(jnp.mean(x * x, axis=-1, keepdims=True) + EPS) * nw_ref[hh:hh + 1, :]
            o_ref[:, obase + hh * MLA_HEAD_PAD: obase + hh * MLA_HEAD_PAD + MLA_NOPE] = y.astype(BF16)
        r = jnp.where(first, t1, t2)
        r2 = r * r
        ss_a = jnp.sum(jnp.where(first, r2, 0.0), axis=-1, keepdims=True)
        ss_b = jnp.sum(jnp.where(first, 0.0, r2), axis=-1, keepdims=True)
        inv = jnp.where(first, lax.rsqrt(ss_a * (1.0 / MLA_ROPE) + EPS), lax.rsqrt(ss_b * (1.0 / MLA_ROPE) + EPS))
        rn = r * inv * rw_ref[...]
        if rope:
            rn = _rope_tile(rn, cos_ref[...], sin_ref[...])
        o_ref[:, obase + MLA_NOPE: obase + MLA_HEAD_PAD] = jnp.where(first, rn, 0.0).astype(BF16)
        o_ref[:, obase + MLA_HEAD_PAD + MLA_NOPE: obase + 2 * MLA_HEAD_PAD] = jnp.where(first, 0.0, rn).astype(BF16)


def mla_queries(cqn, w_uq, q_nope_w, q_rope_w2, rope_tables, *, layer):
    m, k = cqn.shape
    tm = 1024
    tn = PAIRS_PER_TILE * PAIR_W
    to = PAIRS_PER_TILE * 2 * MLA_HEAD_PAD
    rope = rope_tables is not None
    in_specs = [
        pl.BlockSpec((tm, k), lambda i, j: (i, 0)),
        pl.BlockSpec((None, k, tn), lambda i, j: (layer, 0, j)),
        pl.BlockSpec((2, MLA_NOPE), lambda i, j: (0, 0)),
        pl.BlockSpec((1, LANES), lambda i, j: (0, 0)),
    ]
    args = [cqn, w_uq, q_nope_w, q_rope_w2]
    if rope:
        nt = rope_tables[0].shape[0] // tm
        in_specs += [pl.BlockSpec((tm, LANES), lambda i, j: (i % nt, 0))] * 2
        args += list(rope_tables)
    return pl.pallas_call(
        functools.partial(_mla_q_kernel, rope=rope),
        grid=(m // tm, w_uq.shape[2] // tn),
        in_specs=in_specs,
        out_specs=pl.BlockSpec((tm, to), lambda i, j: (i, j)),
        out_shape=jax.ShapeDtypeStruct((m, MLA_HEADS * MLA_HEAD_PAD), BF16),
        name="mla_queries",
        compiler_params=_params(("parallel", "parallel")),
    )(*args)


KV_HEADS_PER_TILE = 8


def _mla_kv_kernel(x_ref, w_ref, kr2_ref, nw_ref, k_ref, v_ref):
    pieces = []
    for hh in range(KV_HEADS_PER_TILE):
        base = hh * (MLA_NOPE + MLA_V)
        wk = w_ref[:, base: base + MLA_NOPE]
        pieces += [pltpu.roll(wk, MLA_NOPE // 2, 1) if hh % 2 else wk, w_ref[:, base + MLA_NOPE: base + MLA_NOPE + MLA_V]]
    w = jnp.concatenate(pieces, axis=1).astype(BF16)
    acc = jnp.dot(x_ref[...], w, preferred_element_type=F32)
    kr2 = kr2_ref[...]
    for hh in range(KV_HEADS_PER_TILE):
        base = hh * (MLA_NOPE + MLA_V)
        x = acc[:, base: base + MLA_NOPE]
        y = x * lax.rsqrt(jnp.mean(x * x, axis=-1, keepdims=True) + EPS) * nw_ref[hh % 2:hh % 2 + 1, :]
        k_ref[:, hh * MLA_HEAD_PAD: hh * MLA_HEAD_PAD + MLA_NOPE] = y.astype(BF16)
        k_ref[:, hh * MLA_HEAD_PAD + MLA_NOPE: (hh + 1) * MLA_HEAD_PAD] = kr2
        v_ref[:, hh * MLA_V: (hh + 1) * MLA_V] = acc[:, base + MLA_NOPE: base + MLA_NOPE + MLA_V].astype(BF16)


def mla_keys_values(ckv, kr2, w_ukv, k_nope_w, *, layer, tm):
    m, k = ckv.shape
    tn = KV_HEADS_PER_TILE * (MLA_NOPE + MLA_V)
    return pl.pallas_call(
        _mla_kv_kernel,
        grid=(m // tm, w_ukv.shape[2] // tn),
        in_specs=[
            pl.BlockSpec((tm, k), lambda i, j: (i, 0)),
            pl.BlockSpec((None, k, tn), lambda i, j: (layer, 0, j)),
            pl.BlockSpec((tm, LANES), lambda i, j: (i, 0)),
            pl.BlockSpec((2, MLA_NOPE), lambda i, j: (0, 0)),
        ],
        out_specs=[
            pl.BlockSpec((tm, KV_HEADS_PER_TILE * MLA_HEAD_PAD), lambda i, j: (i, j)),
            pl.BlockSpec((tm, KV_HEADS_PER_TILE * MLA_V), lambda i, j: (i, j)),
        ],
        out_shape=[
            jax.ShapeDtypeStruct((m, MLA_HEADS * MLA_HEAD_PAD), BF16),
            jax.ShapeDtypeStruct((m, MLA_HEADS * MLA_V), BF16),
        ],
        name="mla_keys_values",
        compiler_params=_params(("parallel", "parallel"), VMEM_LIMIT_BIG),
    )(ckv, w_ukv, kr2, k_nope_w)


def _attn_kernel(q_ref, k_ref, v_ref, o_ref, *, heads):
    exp2_scale = float(MLA_QK_DIM ** -0.5 * np.log2(np.e))
    for hh in range(heads):
        q = q_ref[:, hh * MLA_HEAD_PAD:(hh + 1) * MLA_HEAD_PAD]
        k = k_ref[:, hh * MLA_HEAD_PAD:(hh + 1) * MLA_HEAD_PAD]
        s = lax.dot_general(q, k, NT_DIMS, preferred_element_type=F32)
        p = jnp.exp2((s - jnp.max(s, axis=-1, keepdims=True)) * exp2_scale)
        o = jnp.dot(p.astype(BF16), v_ref[:, hh * MLA_V:(hh + 1) * MLA_V], preferred_element_type=F32)
        o_ref[:, hh * MLA_V:(hh + 1) * MLA_V] = (o / jnp.sum(p, axis=-1, keepdims=True)).astype(BF16)


def attention(q, k, v, *, batch, heads_per_step, tq):
    m = q.shape[0]
    t = m // batch
    s = k.shape[0] // batch
    nq = t // tq
    hb = heads_per_step
    return pl.pallas_call(
        functools.partial(_attn_kernel, heads=hb),
        grid=(batch, MLA_HEADS // hb, nq),
        in_specs=[
            pl.BlockSpec((tq, hb * MLA_HEAD_PAD), lambda b, g, i: (b * nq + i, g)),
            pl.BlockSpec((s, hb * MLA_HEAD_PAD), lambda b, g, i: (b, g)),
            pl.BlockSpec((s, hb * MLA_V), lambda b, g, i: (b, g)),
        ],
        out_specs=pl.BlockSpec((tq, hb * MLA_V), lambda b, g, i: (b * nq + i, g)),
        out_shape=jax.ShapeDtypeStruct((m, MLA_HEADS * MLA_V), BF16),
        name="attention",
        compiler_params=_params(("parallel", "parallel", "parallel"), VMEM_LIMIT_BIG),
    )(q, k, v)


def _rope_tables(n_tok):
    pos = np.arange(n_tok)
    row = (pos // GRID_W).astype(np.float32)
    col = (pos % GRID_W).astype(np.float32)
    inv_freq = jnp.asarray(ROPE_BASE, F32) ** (-jnp.arange(0, ROPE_AXIS_DIM, 2, dtype=F32) / ROPE_AXIS_DIM)
    lane = np.arange(LANES)
    freq = inv_freq[lane % ROPE_HALF]
    use_row = (lane % MLA_ROPE) < ROPE_AXIS_DIM
    ang = jnp.where(use_row[None, :], row[:, None], col[:, None]) * freq[None, :]
    sign = np.where(lane % ROPE_AXIS_DIM < ROPE_HALF, -1.0, 1.0).astype(np.float32)
    return jnp.cos(ang), jnp.sin(ang) * sign[None, :]


def _mlstm_layer(h, x, mod, w_in_t, gate_b, hnorm_w, w_out, *, layer, batch, state, emit_state, mod_kw):
    tn = 512
    u_qkv, o = matmul_split(h, w_in_t, layer=layer, n_out1=2 * MLSTM_QK_W + MLSTM_V_W, dtype1=BF16,
                            n_out2=MLSTM_V_W, dtype2=F32, tn=tn)
    g, gt = mlstm_gates(h, w_in_t, gate_b[layer], layer=layer)
    outs = mlstm_core(u_qkv, g, gt, batch=batch, state=state, emit_state=emit_state,
                      heads_per_step=MLSTM_HEADS if state is None else MLSTM_HEADS // 2)
    n_h = len(outs) - (3 if emit_state else 0)
    hg = mlstm_hnorm_gate(outs[:n_h], o, hnorm_w[layer])
    x = matmul(hg, w_out, layer=layer, n_out=D_MODEL, tn=512, out_dtype=F32, epilogue="resid", res=x, mod=mod,
               gate_piece=2, **mod_kw)
    return x, outs[n_h:]


def _mla_layer(h, x, mod, wts, *, layer, batch, ctx, rope_tables, mod_kw):
    w_in_t, qnorm_w, kvnorm_w, w_uq, w_ukv, q_nope_w, q_rope_w2, k_nope_w, kr_w, w_out = wts
    a = matmul(h, w_in_t, layer=layer, w_is_nk=True, n_out=MLA_IN_PAD, tn=512, out_dtype=F32)
    cqn, ckvn, krn, kr2 = mla_latents(a, qnorm_w, kvnorm_w, kr_w, rope_tables)
    q = mla_queries(cqn, w_uq, q_nope_w, q_rope_w2, rope_tables, layer=layer)
    ckv_all, kr2_all = ckvn.astype(BF16), kr2
    if ctx is not None:
        ckv_ctx, kr_ctx = ctx
        t = ckvn.shape[0] // batch
        kr2_ctx = jnp.concatenate([kr_ctx, kr_ctx], axis=-1).astype(BF16)
        ckv_all = jnp.concatenate([ckv_ctx.astype(BF16), ckv_all.reshape(batch, t, -1)], axis=1).reshape(-1, MLA_KV_LORA)
        kr2_all = jnp.concatenate([kr2_ctx, kr2.reshape(batch, t, -1)], axis=1).reshape(-1, LANES)
    k, v = mla_keys_values(ckv_all, kr2_all, w_ukv, k_nope_w, layer=layer, tm=1024 if ctx is None else 1536)
    if ctx is None:
        o = attention(q, k, v, batch=batch, heads_per_step=16, tq=q.shape[0] // batch)
    else:
        o = attention(q, k, v, batch=batch, heads_per_step=8, tq=512)
    x = matmul(o, w_out, layer=layer, n_out=D_MODEL, tn=512, out_dtype=F32, epilogue="resid", res=x, mod=mod,
               gate_piece=2, **mod_kw)
    return x, ckvn, krn


def _mlp(x, mod, norm_w, w1, w2, *, layer, mod_kw):
    h = norm_modulate(x, norm_w[layer], mod, 3, **mod_kw)
    u = matmul(h, w1, layer=layer, n_out=D_FF, tn=512, out_dtype=BF16, epilogue="relu2", x_buffers=2)
    return matmul_ksplit_resid(u, w2, x, mod, layer=layer, gate_piece=5, **mod_kw)


def kernel(x_prompt, x_sample, state_mlstm_C, state_mlstm_n, state_mlstm_m, cache_mla_ckv, cache_mla_krope, c, c_ctx,
           norm1_w, norm2_w, mod_w, mod_b, mlp_w1, mlp_w2,
           mlstm_w_in, mlstm_gate_b, mlstm_hnorm_w, mlstm_w_out,
           mla_w_in, mla_qnorm_w, mla_kvnorm_w, mla_w_uq, mla_w_ukv, mla_q_norm_w, mla_k_norm_w, mla_w_out):
    bp, tp, d = x_prompt.shape
    bs, ts, _ = x_sample.shape
    xp = x_prompt.reshape(bp * tp, d)
    xs = x_sample.reshape(bs * ts, d)
    cond = jnp.concatenate([c_ctx[None, :], c, jnp.zeros((MOD_ROWS - 1 - bs, d), F32)], axis=0)
    mod = modulation_table(cond, mod_w, mod_b)
    mod0, mod1 = mod[0], mod[1]
    kw_p = dict(rows_per_mod=bp * tp, mod_base=0)
    kw_s = dict(rows_per_mod=ts, mod_base=1)

    wa = (jnp.swapaxes(mlstm_w_in, 1, 2), mlstm_gate_b, mlstm_hnorm_w, mlstm_w_out)
    hp = norm_modulate(xp, norm1_w[0], mod0, 0, **kw_p)
    hs = norm_modulate(xs, norm1_w[0], mod0, 0, **kw_s)
    xp, (c_new, n_new, m_new) = _mlstm_layer(hp, xp, mod0, *wa, layer=0, batch=bp, state=None, emit_state=True,
                                             mod_kw=kw_p)
    state = (state_mlstm_C[:, 0], state_mlstm_n[:, 0], state_mlstm_m[:, 0])
    xs, _ = _mlstm_layer(hs, xs, mod0, *wa, layer=0, batch=bs, state=state, emit_state=False, mod_kw=kw_s)
    xp = _mlp(xp, mod0, norm2_w, mlp_w1, mlp_w2, layer=0, mod_kw=kw_p)
    xs = _mlp(xs, mod0, norm2_w, mlp_w1, mlp_w2, layer=0, mod_kw=kw_s)

    qw, kw = mla_q_norm_w[0], mla_k_norm_w[0]
    pad = lambda w: jnp.concatenate([w, jnp.zeros_like(w)]).reshape(1, LANES)
    both_orders = lambda w: jnp.stack([w, jnp.roll(w, MLA_NOPE // 2)])
    wb = (
        jnp.swapaxes(mla_w_in, 1, 2), mla_qnorm_w[0], mla_kvnorm_w[0], mla_w_uq, mla_w_ukv,
        both_orders(qw[:MLA_NOPE]), jnp.concatenate([qw[MLA_NOPE:], qw[MLA_NOPE:]]).reshape(1, LANES),
        both_orders(kw[:MLA_NOPE]), pad(kw[MLA_NOPE:]), mla_w_out,
    )
    hp = norm_modulate(xp, norm1_w[1], mod1, 0, **kw_p)
    hs = norm_modulate(xs, norm1_w[1], mod1, 0, **kw_s)
    xp, ckv_p, kr_p = _mla_layer(hp, xp, mod1, wb, layer=0, batch=bp, ctx=None, rope_tables=None, mod_kw=kw_p)
    xs, _, _ = _mla_layer(hs, xs, mod1, wb, layer=0, batch=bs, ctx=(cache_mla_ckv[:, 0], cache_mla_krope[:, 0]),
                          rope_tables=_rope_tables(ts), mod_kw=kw_s)
    xp = _mlp(xp, mod1, norm2_w, mlp_w1, mlp_w2, layer=1, mod_kw=kw_p)
    xs = _mlp(xs, mod1, norm2_w, mlp_w1, mlp_w2, layer=1, mod_kw=kw_s)

    return (
        xp.reshape(bp, tp, d),
        xs.reshape(bs, ts, d),
        c_new.reshape(bp, 1, 2, MLSTM_HEADS, MLSTM_DK, MLSTM_DV),
        n_new.reshape(bp, 1, 2, MLSTM_HEADS, MLSTM_DK),
        m_new[..., 0, 0].reshape(bp, 1, 2, MLSTM_HEADS),
        ckv_p.reshape(bp, 1, tp, MLA_KV_LORA),
        kr_p[:, :MLA_ROPE].reshape(bp, 1, tp, MLA_ROPE),
    )
```

```python
import functools

import jax
import jax.numpy as jnp
import numpy as np
from jax import lax
from jax.experimental import pallas as pl
from jax.experimental.pallas import tpu as pltpu

F32 = jnp.float32
BF16 = jnp.bfloat16

D_MODEL = 4096
D_FF = 4 * D_MODEL
N_MOD = 6
EPS = 1e-6
GRID_W = 64

MLSTM_HEADS = 8
MLSTM_DK = 256
MLSTM_DV = 512
MLSTM_QK_W = MLSTM_HEADS * MLSTM_DK
MLSTM_V_W = MLSTM_HEADS * MLSTM_DV
MLSTM_N_GATES = 4 * MLSTM_HEADS
MLSTM_CHUNK = 256

MLA_HEADS = 32
MLA_NOPE = 128
MLA_ROPE = 64
MLA_V = 128
MLA_QK_DIM = MLA_NOPE + MLA_ROPE
MLA_Q_LORA = 1024
MLA_KV_LORA = 512
MLA_IN_PAD = 2048
MLA_HEAD_PAD = 256
ROPE_AXIS_DIM = MLA_ROPE // 2
ROPE_HALF = ROPE_AXIS_DIM // 2
ROPE_BASE = 10000.0

LANES = 128
SUBLANES = 8
MOD_ROWS = SUBLANES
VMEM_LIMIT_BIG = 58 * 1024 * 1024
VMEM_LIMIT_MED = 40 * 1024 * 1024

NT_DIMS = (((1,), (1,)), ((), ()))
TN_DIMS = (((0,), (0,)), ((), ()))


def _params(semantics, vmem=VMEM_LIMIT_MED):
    return pltpu.CompilerParams(dimension_semantics=semantics, vmem_limit_bytes=vmem)


def _mod_row(tile, tm, rows_per_mod, mod_base):
    return mod_base + (tile * tm) // rows_per_mod


def _mod_kernel(c_ref, w_ref, b_ref, o_ref):
    c = c_ref[...]
    a = (c * (1.0 / (1.0 + jnp.exp(-c)))).astype(BF16)
    o_ref[0] = jnp.dot(a, w_ref[0].astype(BF16), preferred_element_type=F32) + b_ref[0]


def modulation_table(cond, mod_w, mod_b):
    depth, d, n = mod_w.shape
    tn = 1024
    return pl.pallas_call(
        _mod_kernel,
        grid=(depth, n // tn),
        in_specs=[
            pl.BlockSpec((MOD_ROWS, d), lambda l, j: (0, 0)),
            pl.BlockSpec((1, d, tn), lambda l, j: (l, 0, j)),
            pl.BlockSpec((1, 1, tn), lambda l, j: (l, 0, j)),
        ],
        out_specs=pl.BlockSpec((1, MOD_ROWS, tn), lambda l, j: (l, 0, j)),
        out_shape=jax.ShapeDtypeStruct((depth, MOD_ROWS, n), F32),
        name="modulation_table",
        compiler_params=_params(("parallel", "parallel"), VMEM_LIMIT_BIG),
    )(cond, mod_w, mod_b.reshape(depth, 1, n))


def _normmod_kernel(x_ref, nw_ref, shift_ref, scale_ref, o_ref, *, tm, rows_per_mod, mod_base):
    x = x_ref[...]
    row = _mod_row(pl.program_id(0), tm, rows_per_mod, mod_base)
    shift = shift_ref[pl.ds(row, 1), :]
    gain = nw_ref[...] * (1.0 + scale_ref[pl.ds(row, 1), :])
    y = x * lax.rsqrt(jnp.mean(x * x, axis=-1, keepdims=True) + EPS)
    o_ref[...] = (y * gain + shift).astype(BF16)


def norm_modulate(x, norm_w, mod, piece, *, rows_per_mod, mod_base):
    m, d = x.shape
    tm = 512
    return pl.pallas_call(
        functools.partial(_normmod_kernel, tm=tm, rows_per_mod=rows_per_mod, mod_base=mod_base),
        grid=(m // tm,),
        in_specs=[
            pl.BlockSpec((tm, d), lambda i: (i, 0)),
            pl.BlockSpec((1, d), lambda i: (0, 0)),
            pl.BlockSpec((MOD_ROWS, d), lambda i: (0, piece)),
            pl.BlockSpec((MOD_ROWS, d), lambda i: (0, piece + 1)),
        ],
        out_specs=pl.BlockSpec((tm, d), lambda i: (i, 0)),
        out_shape=jax.ShapeDtypeStruct((m, d), BF16),
        name="norm_modulate",
        compiler_params=_params(("parallel",)),
    )(x, norm_w.reshape(1, d), mod, mod)


def _mm_kernel(*refs, epilogue, w_is_nk, n_valid, tm, rows_per_mod, mod_base):
    if epilogue == "resid":
        x_ref, w_ref, res_ref, gate_ref, o_ref = refs
    else:
        x_ref, w_ref, o_ref = refs
    if w_is_nk:
        acc = lax.dot_general(x_ref[...], w_ref[...].astype(BF16), NT_DIMS, preferred_element_type=F32)
    else:
        acc = jnp.dot(x_ref[...], w_ref[...].astype(BF16), preferred_element_type=F32)
    if n_valid is not None:
        col = pl.program_id(1) * acc.shape[1] + lax.broadcasted_iota(jnp.int32, acc.shape, 1)
        acc = jnp.where(col < n_valid, acc, 0.0)
    if epilogue == "relu2":
        r = jnp.maximum(acc, 0.0)
        o_ref[...] = (r * r).astype(o_ref.dtype)
    elif epilogue == "resid":
        row = _mod_row(pl.program_id(0), tm, rows_per_mod, mod_base)
        o_ref[...] = res_ref[...] + gate_ref[pl.ds(row, 1), :] * acc
    else:
        o_ref[...] = acc.astype(o_ref.dtype)


def matmul(x, w, *, n_out, tn, out_dtype, layer=0, w_is_nk=False, epilogue=None, res=None, mod=None,
           gate_piece=0, rows_per_mod=1, mod_base=0, tm=2048, x_buffers=1):
    m, k = x.shape
    n_w = w.shape[1] if w_is_nk else w.shape[2]
    n_valid = n_w if n_w < n_out else None
    if w_is_nk:
        w_spec = pl.BlockSpec((None, tn, k), lambda i, j: (layer, j, 0))
    else:
        w_spec = pl.BlockSpec((None, k, tn), lambda i, j: (layer, 0, j))
    in_specs = [pl.BlockSpec((tm, k), lambda i, j: (i, 0), pipeline_mode=pl.Buffered(x_buffers)), w_spec]
    args = [x, w]
    if epilogue == "resid":
        gate_off = gate_piece * (D_MODEL // tn)
        in_specs += [
            pl.BlockSpec((tm, tn), lambda i, j: (i, j)),
            pl.BlockSpec((MOD_ROWS, tn), lambda i, j: (0, gate_off + j)),
        ]
        args += [res, mod]
    return pl.pallas_call(
        functools.partial(_mm_kernel, epilogue=epilogue, w_is_nk=w_is_nk, n_valid=n_valid, tm=tm,
                          rows_per_mod=rows_per_mod, mod_base=mod_base),
        grid=(m // tm, n_out // tn),
        in_specs=in_specs,
        out_specs=pl.BlockSpec((tm, tn), lambda i, j: (i, j)),
        out_shape=jax.ShapeDtypeStruct((m, n_out), out_dtype),
        name=f"matmul_{epilogue or 'plain'}",
        compiler_params=_params(("parallel", "parallel"), VMEM_LIMIT_BIG),
    )(*args)


def _mm_split_kernel(x_ref, w_ref, o1_ref, o2_ref, *, n1):
    j = pl.program_id(1)

    def prod():
        return lax.dot_general(x_ref[...], w_ref[...].astype(BF16), NT_DIMS, preferred_element_type=F32)

    @pl.when(j < n1)
    def _():
        o1_ref[...] = prod().astype(o1_ref.dtype)

    @pl.when(j >= n1)
    def _():
        o2_ref[...] = prod().astype(o2_ref.dtype)


def matmul_split(x, w_t, *, layer, n_out1, dtype1, n_out2, dtype2, tn, tm=2048):
    m, k = x.shape
    n1, n2 = n_out1 // tn, n_out2 // tn
    return pl.pallas_call(
        functools.partial(_mm_split_kernel, n1=n1),
        grid=(m // tm, n1 + n2),
        in_specs=[
            pl.BlockSpec((tm, k), lambda i, j: (i, 0), pipeline_mode=pl.Buffered(1)),
            pl.BlockSpec((None, tn, k), lambda i, j: (layer, j, 0)),
        ],
        out_specs=[
            pl.BlockSpec((tm, tn), lambda i, j: (i, jnp.minimum(j, n1 - 1))),
            pl.BlockSpec((tm, tn), lambda i, j: (i, jnp.maximum(j - n1, 0))),
        ],
        out_shape=[jax.ShapeDtypeStruct((m, n_out1), dtype1), jax.ShapeDtypeStruct((m, n_out2), dtype2)],
        name="matmul_split",
        compiler_params=_params(("parallel", "arbitrary"), VMEM_LIMIT_BIG),
    )(x, w_t)


def _mm_ksplit_kernel(x_ref, w_ref, res_ref, gate_ref, o_ref, *, nk, tm, rows_per_mod, mod_base):
    k = pl.program_id(2)
    row = _mod_row(pl.program_id(0), tm, rows_per_mod, mod_base)

    def part():
        return gate_ref[pl.ds(row, 1), :] * jnp.dot(x_ref[...], w_ref[...].astype(BF16),
                                                    preferred_element_type=F32)

    @pl.when(k == 0)
    def _():
        o_ref[...] = part()

    @pl.when(k > 0)
    def _():
        o_ref[...] = o_ref[...] + part()

    slab = tm // nk
    rows = pl.ds(pl.multiple_of(k * slab, slab), slab)
    o_ref[rows, :] = o_ref[rows, :] + res_ref[...]


def matmul_ksplit_resid(x, w, res, mod, *, layer, gate_piece, rows_per_mod, mod_base, tm=2048, tn=1024, tk=2048):
    m, kdim = x.shape
    n = w.shape[2]
    nk = kdim // tk
    gate_off = gate_piece * (D_MODEL // tn)
    return pl.pallas_call(
        functools.partial(_mm_ksplit_kernel, nk=nk, tm=tm, rows_per_mod=rows_per_mod, mod_base=mod_base),
        grid=(m // tm, n // tn, nk),
        in_specs=[
            pl.BlockSpec((tm, tk), lambda i, j, k: (i, k)),
            pl.BlockSpec((None, tk, tn), lambda i, j, k: (layer, k, j)),
            pl.BlockSpec((tm // nk, tn), lambda i, j, k: (i * nk + k, j)),
            pl.BlockSpec((MOD_ROWS, tn), lambda i, j, k: (0, gate_off + j)),
        ],
        out_specs=pl.BlockSpec((tm, tn), lambda i, j, k: (i, j), pipeline_mode=pl.Buffered(1)),
        out_shape=jax.ShapeDtypeStruct((m, n), F32),
        name="matmul_ksplit_resid",
        compiler_params=_params(("parallel", "parallel", "arbitrary"), VMEM_LIMIT_BIG),
    )(x, w, res, mod)


def _log_sigmoid(x):
    return jnp.minimum(x, 0.0) - jnp.log(1.0 + jnp.exp(-jnp.abs(x)))


def _gate_kernel(h_ref, wg_ref, b_ref, g_ref, gt_ref):
    gt = lax.dot_general(wg_ref[...].astype(BF16), h_ref[...], NT_DIMS, preferred_element_type=F32)
    row = lax.broadcasted_iota(jnp.int32, gt.shape, 0)
    gt = jnp.where(row < MLSTM_N_GATES, gt, 0.0) + b_ref[...]
    gt = jnp.where((row // MLSTM_HEADS) % 2 == 1, _log_sigmoid(gt), gt)
    gt_ref[...] = gt[:MLSTM_N_GATES, :]
    g_ref[...] = gt.T[:, :MLSTM_N_GATES]


def mlstm_gates(h, w_in_t, gate_b, *, layer):
    m, d = h.shape
    ng = MLSTM_N_GATES
    tm = 1024
    gate_blk = (2 * MLSTM_QK_W + 2 * MLSTM_V_W) // LANES
    bias = jnp.zeros((LANES, 1), F32).at[:ng, 0].set(gate_b)
    return pl.pallas_call(
        _gate_kernel,
        grid=(m // tm,),
        in_specs=[
            pl.BlockSpec((tm, d), lambda i: (i, 0)),
            pl.BlockSpec((None, LANES, d), lambda i: (layer, gate_blk, 0)),
            pl.BlockSpec((LANES, 1), lambda i: (0, 0)),
        ],
        out_specs=[pl.BlockSpec((tm, ng), lambda i: (i, 0)), pl.BlockSpec((ng, tm), lambda i: (0, i))],
        out_shape=[jax.ShapeDtypeStruct((m, ng), F32), jax.ShapeDtypeStruct((ng, m), F32)],
        name="mlstm_gates",
        compiler_params=_params(("parallel",)),
    )(h, w_in_t, bias)


def _mlstm_direction(q, k, v, i_col, f_col, i_row, f_row, c_ref, n_ref, m_ref, *, causal, has_prev):
    L = q.shape[0]
    t_idx = lax.broadcasted_iota(jnp.int32, (L, L), 0)
    s_idx = lax.broadcasted_iota(jnp.int32, (L, L), 1)
    if causal:
        mask, mask_t = s_idx <= t_idx, t_idx <= s_idx
    else:
        mask, mask_t = s_idx >= t_idx, t_idx >= s_idx
    b_col = jnp.sum(jnp.where(mask, f_row, 0.0), axis=1, keepdims=True)
    b_row = jnp.sum(jnp.where(mask_t, f_col, 0.0), axis=0, keepdims=True)
    d = jnp.where(mask, b_col - b_row + i_row, -jnp.inf)
    m_prev = m_ref[...] if has_prev else jnp.zeros((1, 1), F32)
    m_inter = b_col + m_prev
    m_t = jnp.maximum(m_inter, jnp.max(d, axis=1, keepdims=True))
    s = lax.dot_general(q, k, NT_DIMS, preferred_element_type=F32) * jnp.exp(d - m_t)
    num = jnp.dot(s.astype(BF16), v, preferred_element_type=F32)
    den = jnp.sum(s, axis=1, keepdims=True)
    if has_prev:
        a_inter = jnp.exp(m_inter - m_t)
        num = num + a_inter * jnp.dot(q, c_ref[...].astype(BF16), preferred_element_type=F32)
        den = den + a_inter * jnp.sum(q.astype(F32) * n_ref[...], axis=1, keepdims=True)
    h = num / jnp.maximum(jnp.abs(den), jnp.exp(-m_t))

    b_last = jnp.sum(f_col, axis=0, keepdims=True)
    w_col = b_last - b_col + i_col
    m_new = jnp.maximum(b_last + m_prev, jnp.max(w_col, axis=0, keepdims=True))
    wk = jnp.exp(w_col - m_new) * k.astype(F32)
    c_upd = lax.dot_general(wk.astype(BF16), v, TN_DIMS, preferred_element_type=F32)
    n_upd = jnp.sum(wk, axis=0, keepdims=True)
    if has_prev:
        decay = jnp.exp(b_last + m_prev - m_new)
        c_ref[...] = decay * c_ref[...] + c_upd
        n_ref[...] = decay * n_ref[...] + n_upd
    else:
        c_ref[...] = c_upd
        n_ref[...] = n_upd
    m_ref[...] = m_new
    return h


def _mlstm_kernel(*refs, nc, hps, has_state, emit_state):
    refs = list(refs)
    m0_ref = refs.pop(0) if has_state else None
    qf_ref, kf_ref, vf_ref, gf_ref, gtf_ref = refs[:5]
    refs = refs[5:]
    if nc > 1:
        qb_ref, kb_ref, vb_ref, gb_ref, gtb_ref = refs[:5]
        refs = refs[5:]
    else:
        qb_ref, kb_ref, vb_ref, gb_ref, gtb_ref = qf_ref, kf_ref, vf_ref, gf_ref, gtf_ref
    if has_state:
        c0_ref, n0_ref = refs[:2]
        refs = refs[2:]
    if nc > 1:
        hf_ref, hb_ref = refs[:2]
        refs = refs[2:]
    else:
        hs_ref = refs.pop(0)
    if emit_state:
        cout_ref, nout_ref, mout_ref = refs[:3]
        refs = refs[3:]
    c_sc, n_sc, m_sc = refs

    b = pl.program_id(0)
    head0 = pl.program_id(1) * hps
    c = pl.program_id(2)
    has_prev = has_state or nc > 1
    dk, dv = MLSTM_DK, MLSTM_DV

    if has_prev:
        @pl.when(c == 0)
        def _():
            for dr in range(2):
                for j in range(hps):
                    if has_state:
                        c_sc[dr, j] = c0_ref[0, dr, j]
                        n_sc[dr, j] = n0_ref[0, dr, j]
                        m_sc[dr, j] = jnp.full((1, 1), m0_ref[(b * 2 + dr) * MLSTM_HEADS + head0 + j], F32)
                    else:
                        c_sc[dr, j] = jnp.zeros((dk, dv), F32)
                        n_sc[dr, j] = jnp.zeros((1, dk), F32)
                        m_sc[dr, j] = jnp.zeros((1, 1), F32)

    def gates(g_ref, gt_ref, idx, j):
        g = g_ref[...]
        lane = lax.broadcasted_iota(jnp.int32, g.shape, 1)
        sel = idx * MLSTM_HEADS + head0 + j
        col = jnp.sum(jnp.where(lane == sel, g, 0.0), axis=1, keepdims=True)
        return col, gt_ref[pl.ds(sel, 1), :]

    q_scale = MLSTM_DK ** -0.5
    for j in range(hps):
        outs = []
        for dr, (q_ref, k_ref, v_ref, g_ref, gt_ref) in enumerate(
                ((qf_ref, kf_ref, vf_ref, gf_ref, gtf_ref), (qb_ref, kb_ref, vb_ref, gb_ref, gtb_ref))):
            i_col, i_row = gates(g_ref, gt_ref, 2 * dr, j)
            f_col, f_row = gates(g_ref, gt_ref, 2 * dr + 1, j)
            outs.append(_mlstm_direction(
                q_ref[:, j * dk:(j + 1) * dk] * q_scale, k_ref[:, j * dk:(j + 1) * dk], v_ref[:, j * dv:(j + 1) * dv],
                i_col, f_col, i_row, f_row, c_sc.at[dr, j], n_sc.at[dr, j], m_sc.at[dr, j],
                causal=(dr == 0), has_prev=has_prev))
        if nc > 1:
            hf_ref[:, j * dv:(j + 1) * dv] = outs[0]
            hb_ref[:, j * dv:(j + 1) * dv] = outs[1]
        else:
            hs_ref[:, j * dv:(j + 1) * dv] = outs[0] + outs[1]

    if emit_state:
        @pl.when(c == nc - 1)
        def _():
            for dr in range(2):
                for j in range(hps):
                    cout_ref[0, dr, j] = c_sc[dr, j]
                    nout_ref[0, dr, j] = n_sc[dr, j]
                    mout_ref[0, dr, j] = jnp.broadcast_to(m_sc[dr, j], (1, LANES))


def mlstm_core(u_qkv, g, gt, *, batch, heads_per_step, state=None, emit_state=False):
    m = u_qkv.shape[0]
    L = MLSTM_CHUNK
    nc = m // batch // L
    hps = heads_per_step
    nh, dk, dv = MLSTM_HEADS, hps * MLSTM_DK, hps * MLSTM_DV
    k_blk0 = MLSTM_QK_W // dk
    v_blk0 = 2 * MLSTM_QK_W // dv
    has_state = state is not None

    def specs(chunk_of):
        rb = lambda b, h, c: b * nc + chunk_of(c)
        return [
            pl.BlockSpec((L, dk), lambda b, h, c: (rb(b, h, c), h)),
            pl.BlockSpec((L, dk), lambda b, h, c: (rb(b, h, c), k_blk0 + h)),
            pl.BlockSpec((L, dv), lambda b, h, c: (rb(b, h, c), v_blk0 + h)),
            pl.BlockSpec((L, MLSTM_N_GATES), lambda b, h, c: (rb(b, h, c), 0)),
            pl.BlockSpec((MLSTM_N_GATES, L), lambda b, h, c: (0, rb(b, h, c))),
        ]

    fwd_chunk = lambda c: c
    bwd_chunk = lambda c: nc - 1 - c
    in_specs, args = [], []
    if has_state:
        c0, n0, m0 = state
        in_specs.append(pl.BlockSpec(memory_space=pltpu.SMEM))
        args.append(m0.reshape(-1))
    in_specs += specs(fwd_chunk)
    args += [u_qkv, u_qkv, u_qkv, g, gt]
    if nc > 1:
        in_specs += specs(bwd_chunk)
        args += [u_qkv, u_qkv, u_qkv, g, gt]
    state_specs = [
        pl.BlockSpec((1, 2, hps, MLSTM_DK, MLSTM_DV), lambda b, h, c: (b, 0, h, 0, 0)),
        pl.BlockSpec((1, 2, hps, 1, MLSTM_DK), lambda b, h, c: (b, 0, h, 0, 0)),
    ]
    if has_state:
        in_specs += state_specs
        args += [c0, n0.reshape(batch, 2, nh, 1, MLSTM_DK)]

    h_shape = jax.ShapeDtypeStruct((m, MLSTM_V_W), F32)
    if nc > 1:
        out_specs = [
            pl.BlockSpec((L, dv), lambda b, h, c: (b * nc + fwd_chunk(c), h)),
            pl.BlockSpec((L, dv), lambda b, h, c: (b * nc + bwd_chunk(c), h)),
        ]
        out_shape = [h_shape, h_shape]
    else:
        out_specs = [pl.BlockSpec((L, dv), lambda b, h, c: (b, h))]
        out_shape = [h_shape]
    if emit_state:
        out_specs += state_specs + [pl.BlockSpec((1, 2, hps, 1, LANES), lambda b, h, c: (b, 0, h, 0, 0))]
        out_shape += [
            jax.ShapeDtypeStruct((batch, 2, nh, MLSTM_DK, MLSTM_DV), F32),
            jax.ShapeDtypeStruct((batch, 2, nh, 1, MLSTM_DK), F32),
            jax.ShapeDtypeStruct((batch, 2, nh, 1, LANES), F32),
        ]
    return pl.pallas_call(
        functools.partial(_mlstm_kernel, nc=nc, hps=hps, has_state=has_state, emit_state=emit_state),
        grid=(batch, nh // hps, nc),
        in_specs=in_specs,
        out_specs=out_specs,
        out_shape=out_shape,
        scratch_shapes=[pltpu.VMEM((2, hps, MLSTM_DK, MLSTM_DV), F32), pltpu.VMEM((2, hps, 1, MLSTM_DK), F32),
                        pltpu.VMEM((2, hps, 1, 1), F32)],
        name="mlstm_core",
        compiler_params=_params(("parallel", "parallel", "arbitrary"), VMEM_LIMIT_BIG),
    )(*args)


def _hnorm_kernel(*refs, n_in):
    h_refs, (o_ref, w_ref, out_ref) = refs[:n_in], refs[n_in:]
    for hd in range(MLSTM_HEADS):
        sl = slice(hd * MLSTM_DV, (hd + 1) * MLSTM_DV)
        x = h_refs[0][:, sl]
        for r in h_refs[1:]:
            x = x + r[:, sl]
        y = x * lax.rsqrt(jnp.mean(x * x, axis=-1, keepdims=True) + EPS) * w_ref[:, sl]
        gate = 0.5 * jnp.tanh(0.5 * o_ref[:, sl]) + 0.5
        out_ref[:, sl] = (y * gate).astype(BF16)


def mlstm_hnorm_gate(h_parts, o, hnorm_w):
    m, w = o.shape
    tm = 256
    spec = pl.BlockSpec((tm, w), lambda i: (i, 0))
    return pl.pallas_call(
        functools.partial(_hnorm_kernel, n_in=len(h_parts)),
        grid=(m // tm,),
        in_specs=[spec] * len(h_parts) + [spec, pl.BlockSpec((1, w), lambda i: (0, 0))],
        out_specs=spec,
        out_shape=jax.ShapeDtypeStruct((m, w), BF16),
        name="mlstm_hnorm_gate",
        compiler_params=_params(("parallel",)),
    )(*h_parts, o, hnorm_w.reshape(1, w))


def _rope_tile(y, cos, sin_signed):
    lane = lax.broadcasted_iota(jnp.int32, y.shape, 1)
    partner = jnp.where(lane % ROPE_AXIS_DIM < ROPE_HALF,
                        pltpu.roll(y, LANES - ROPE_HALF, 1), pltpu.roll(y, ROPE_HALF, 1))
    return y * cos + partner * sin_signed


def _mla_lat_kernel(*refs, rope):
    if rope:
        a_ref, qw_ref, kvw_ref, krw_ref, cos_ref, sin_ref, cq_ref, ckv_ref, kr_ref, kr2_ref = refs
    else:
        a_ref, qw_ref, kvw_ref, krw_ref, cq_ref, ckv_ref, kr_ref, kr2_ref = refs
    cq = a_ref[:, :MLA_Q_LORA]
    cq_ref[...] = (cq * lax.rsqrt(jnp.mean(cq * cq, axis=-1, keepdims=True) + EPS) * qw_ref[...]).astype(BF16)
    ckv = a_ref[:, MLA_Q_LORA:MLA_Q_LORA + MLA_KV_LORA]
    ckv_ref[...] = ckv * lax.rsqrt(jnp.mean(ckv * ckv, axis=-1, keepdims=True) + EPS) * kvw_ref[...]
    kr = a_ref[:, MLA_Q_LORA + MLA_KV_LORA:MLA_Q_LORA + MLA_KV_LORA + LANES]
    ms = jnp.sum(kr * kr, axis=-1, keepdims=True) * (1.0 / MLA_ROPE)
    krn = kr * lax.rsqrt(ms + EPS) * krw_ref[...]
    kr_ref[...] = krn
    kr2 = krn + pltpu.roll(krn, MLA_ROPE, 1)
    if rope:
        kr2 = _rope_tile(kr2, cos_ref[...], sin_ref[...])
    kr2_ref[...] = kr2.astype(BF16)


def mla_latents(a, qnorm_w, kvnorm_w, kr_w, rope_tables):
    m = a.shape[0]
    tm = 512
    rope = rope_tables is not None
    in_specs = [
        pl.BlockSpec((tm, MLA_IN_PAD), lambda i: (i, 0)),
        pl.BlockSpec((1, MLA_Q_LORA), lambda i: (0, 0)),
        pl.BlockSpec((1, MLA_KV_LORA), lambda i: (0, 0)),
        pl.BlockSpec((1, LANES), lambda i: (0, 0)),
    ]
    args = [a, qnorm_w.reshape(1, -1), kvnorm_w.reshape(1, -1), kr_w]
    if rope:
        nt = rope_tables[0].shape[0] // tm
        in_specs += [pl.BlockSpec((tm, LANES), lambda i: (i % nt, 0))] * 2
        args += list(rope_tables)
    return pl.pallas_call(
        functools.partial(_mla_lat_kernel, rope=rope),
        grid=(m // tm,),
        in_specs=in_specs,
        out_specs=[
            pl.BlockSpec((tm, MLA_Q_LORA), lambda i: (i, 0)),
            pl.BlockSpec((tm, MLA_KV_LORA), lambda i: (i, 0)),
            pl.BlockSpec((tm, LANES), lambda i: (i, 0)),
            pl.BlockSpec((tm, LANES), lambda i: (i, 0)),
        ],
        out_shape=[
            jax.ShapeDtypeStruct((m, MLA_Q_LORA), BF16),
            jax.ShapeDtypeStruct((m, MLA_KV_LORA), F32),
            jax.ShapeDtypeStruct((m, LANES), F32),
            jax.ShapeDtypeStruct((m, LANES), BF16),
        ],
        name="mla_latents",
        compiler_params=_params(("parallel",)),
    )(*args)


PAIR_W = 2 * MLA_QK_DIM
PAIRS_PER_TILE = 2


def _mla_q_kernel(*refs, rope):
    if rope:
        x_ref, w_ref, nw_ref, rw_ref, cos_ref, sin_ref, o_ref = refs
    else:
        x_ref, w_ref, nw_ref, rw_ref, o_ref = refs
    acc = jnp.dot(x_ref[...], w_ref[...].astype(BF16), preferred_element_type=F32)
    lane = lax.broadcasted_iota(jnp.int32, (acc.shape[0], LANES), 1)
    first = lane < MLA_ROPE
    for p in range(PAIRS_PER_TILE):
        base = p * PAIR_W
        obase = p * 2 * MLA_HEAD_PAD
        t1 = acc[:, base + LANES: base + 2 * LANES]
        t2 = acc[:, base + 2 * LANES: base + PAIR_W]
        for hh, x in enumerate((acc[:, base: base + LANES], jnp.where(first, t2, t1))):
            y = x * lax.rsqrt(jnp.mean(x * x, axis=-1, keepdims=True) + EPS) * nw_ref[hh:hh + 1, :]
            o_ref[:, obase + hh * MLA_HEAD_PAD: obase + hh * MLA_HEAD_PAD + MLA_NOPE] = y.astype(BF16)
        r = jnp.where(first, t1, t2)
        r2 = r * r
        ss_a = jnp.sum(jnp.where(first, r2, 0.0), axis=-1, keepdims=True)
        ss_b = jnp.sum(jnp.where(first, 0.0, r2), axis=-1, keepdims=True)
        inv = jnp.where(first, lax.rsqrt(ss_a * (1.0 / MLA_ROPE) + EPS), lax.rsqrt(ss_b * (1.0 / MLA_ROPE) + EPS))
        rn = r * inv * rw_ref[...]
        if rope:
            rn = _rope_tile(rn, cos_ref[...], sin_ref[...])
        o_ref[:, obase + MLA_NOPE: obase + MLA_HEAD_PAD] = jnp.where(first, rn, 0.0).astype(BF16)
        o_ref[:, obase + MLA_HEAD_PAD + MLA_NOPE: obase + 2 * MLA_HEAD_PAD] = jnp.where(first, 0.0, rn).astype(BF16)


def mla_queries(cqn, w_uq, q_nope_w, q_rope_w2, rope_tables, *, layer):
    m, k = cqn.shape
    tm = 2048
    tn = PAIRS_PER_TILE * PAIR_W
    to = PAIRS_PER_TILE * 2 * MLA_HEAD_PAD
    rope = rope_tables is not None
    in_specs = [
        pl.BlockSpec((tm, k), lambda i, j: (i, 0)),
        pl.BlockSpec((None, k, tn), lambda i, j: (layer, 0, j)),
        pl.BlockSpec((2, MLA_NOPE), lambda i, j: (0, 0)),
        pl.BlockSpec((1, LANES), lambda i, j: (0, 0)),
    ]
    args = [cqn, w_uq, q_nope_w, q_rope_w2]
    if rope:
        nt = rope_tables[0].shape[0] // tm
        in_specs += [pl.BlockSpec((tm, LANES), lambda i, j: (i % nt, 0))] * 2
        args += list(rope_tables)
    return pl.pallas_call(
        functools.partial(_mla_q_kernel, rope=rope),
        grid=(m // tm, w_uq.shape[2] // tn),
        in_specs=in_specs,
        out_specs=pl.BlockSpec((tm, to), lambda i, j: (i, j)),
        out_shape=jax.ShapeDtypeStruct((m, MLA_HEADS * MLA_HEAD_PAD), BF16),
        name="mla_queries",
        compiler_params=_params(("parallel", "parallel"), VMEM_LIMIT_BIG),
    )(*args)


KV_HEADS_PER_TILE = 8


def _mla_kv_kernel(x_ref, w_ref, kr2_ref, nw_ref, k_ref, v_ref):
    pieces = []
    for hh in range(KV_HEADS_PER_TILE):
        base = hh * (MLA_NOPE + MLA_V)
        wk = w_ref[:, base: base + MLA_NOPE]
        pieces += [pltpu.roll(wk, MLA_NOPE // 2, 1) if hh % 2 else wk, w_ref[:, base + MLA_NOPE: base + MLA_NOPE + MLA_V]]
    w = jnp.concatenate(pieces, axis=1).astype(BF16)
    acc = jnp.dot(x_ref[...], w, preferred_element_type=F32)
    kr2 = kr2_ref[...]
    for hh in range(KV_HEADS_PER_TILE):
        base = hh * (MLA_NOPE + MLA_V)
        x = acc[:, base: base + MLA_NOPE]
        y = x * lax.rsqrt(jnp.mean(x * x, axis=-1, keepdims=True) + EPS) * nw_ref[hh % 2:hh % 2 + 1, :]
        k_ref[:, hh * MLA_HEAD_PAD: hh * MLA_HEAD_PAD + MLA_NOPE] = y.astype(BF16)
        k_ref[:, hh * MLA_HEAD_PAD + MLA_NOPE: (hh + 1) * MLA_HEAD_PAD] = kr2
        v_ref[:, hh * MLA_V: (hh + 1) * MLA_V] = acc[:, base + MLA_NOPE: base + MLA_NOPE + MLA_V].astype(BF16)


def mla_keys_values(ckv, kr2, w_ukv, k_nope_w, *, layer, tm):
    m, k = ckv.shape
    tn = KV_HEADS_PER_TILE * (MLA_NOPE + MLA_V)
    return pl.pallas_call(
        _mla_kv_kernel,
        grid=(m // tm, w_ukv.shape[2] // tn),
        in_specs=[
            pl.BlockSpec((tm, k), lambda i, j: (i, 0)),
            pl.BlockSpec((None, k, tn), lambda i, j: (layer, 0, j)),
            pl.BlockSpec((tm, LANES), lambda i, j: (i, 0)),
            pl.BlockSpec((2, MLA_NOPE), lambda i, j: (0, 0)),
        ],
        out_specs=[
            pl.BlockSpec((tm, KV_HEADS_PER_TILE * MLA_HEAD_PAD), lambda i, j: (i, j)),
            pl.BlockSpec((tm, KV_HEADS_PER_TILE * MLA_V), lambda i, j: (i, j)),
        ],
        out_shape=[
            jax.ShapeDtypeStruct((m, MLA_HEADS * MLA_HEAD_PAD), BF16),
            jax.ShapeDtypeStruct((m, MLA_HEADS * MLA_V), BF16),
        ],
        name="mla_keys_values",
        compiler_params=_params(("parallel", "parallel"), VMEM_LIMIT_BIG),
    )(ckv, w_ukv, kr2, k_nope_w)


def _attn_kernel(q_ref, k_ref, v_ref, o_ref, *, heads):
    exp2_scale = float(MLA_QK_DIM ** -0.5 * np.log2(np.e))
    for hh in range(heads):
        q = q_ref[:, hh * MLA_HEAD_PAD:(hh + 1) * MLA_HEAD_PAD]
        k = k_ref[:, hh * MLA_HEAD_PAD:(hh + 1) * MLA_HEAD_PAD]
        s = lax.dot_general(q, k, NT_DIMS, preferred_element_type=F32)
        p = jnp.exp2((s - jnp.max(s, axis=-1, keepdims=True)) * exp2_scale)
        o = jnp.dot(p.astype(BF16), v_ref[:, hh * MLA_V:(hh + 1) * MLA_V], preferred_element_type=F32)
        o_ref[:, hh * MLA_V:(hh + 1) * MLA_V] = (o / jnp.sum(p, axis=-1, keepdims=True)).astype(BF16)


def attention(q, k, v, *, batch, heads_per_step, tq):
    m = q.shape[0]
    t = m // batch
    s = k.shape[0] // batch
    nq = t // tq
    hb = heads_per_step
    return pl.pallas_call(
        functools.partial(_attn_kernel, heads=hb),
        grid=(batch, MLA_HEADS // hb, nq),
        in_specs=[
            pl.BlockSpec((tq, hb * MLA_HEAD_PAD), lambda b, g, i: (b * nq + i, g)),
            pl.BlockSpec((s, hb * MLA_HEAD_PAD), lambda b, g, i: (b, g)),
            pl.BlockSpec((s, hb * MLA_V), lambda b, g, i: (b, g)),
        ],
        out_specs=pl.BlockSpec((tq, hb * MLA_V), lambda b, g, i: (b * nq + i, g)),
        out_shape=jax.ShapeDtypeStruct((m, MLA_HEADS * MLA_V), BF16),
        name="attention",
        compiler_params=_params(("parallel", "parallel", "parallel"), VMEM_LIMIT_BIG),
    )(q, k, v)


def _rope_tables(n_tok):
    pos = np.arange(n_tok)
    row = (pos // GRID_W).astype(np.float32)
    col = (pos % GRID_W).astype(np.float32)
    inv_freq = jnp.asarray(ROPE_BASE, F32) ** (-jnp.arange(0, ROPE_AXIS_DIM, 2, dtype=F32) / ROPE_AXIS_DIM)
    lane = np.arange(LANES)
    freq = inv_freq[lane % ROPE_HALF]
    use_row = (lane % MLA_ROPE) < ROPE_AXIS_DIM
    ang = jnp.where(use_row[None, :], row[:, None], col[:, None]) * freq[None, :]
    sign = np.where(lane % ROPE_AXIS_DIM < ROPE_HALF, -1.0, 1.0).astype(np.float32)
    return jnp.cos(ang), jnp.sin(ang) * sign[None, :]


def _mlstm_layer(h, x, mod, w_in_t, gate_b, hnorm_w, w_out, *, layer, batch, state, emit_state, mod_kw):
    tn = 512
    u_qkv, o = matmul_split(h, w_in_t, layer=layer, n_out1=2 * MLSTM_QK_W + MLSTM_V_W, dtype1=BF16,
                            n_out2=MLSTM_V_W, dtype2=F32, tn=tn)
    g, gt = mlstm_gates(h, w_in_t, gate_b[layer], layer=layer)
    outs = mlstm_core(u_qkv, g, gt, batch=batch, state=state, emit_state=emit_state,
                      heads_per_step=MLSTM_HEADS if state is None else MLSTM_HEADS // 2)
    n_h = len(outs) - (3 if emit_state else 0)
    hg = mlstm_hnorm_gate(outs[:n_h], o, hnorm_w[layer])
    x = matmul(hg, w_out, layer=layer, n_out=D_MODEL, tn=512, out_dtype=F32, epilogue="resid", res=x, mod=mod,
               gate_piece=2, **mod_kw)
    return x, outs[n_h:]


def _mla_layer(h, x, mod, wts, *, layer, batch, ctx, rope_tables, mod_kw):
    w_in_t, qnorm_w, kvnorm_w, w_uq, w_ukv, q_nope_w, q_rope_w2, k_nope_w, kr_w, w_out = wts
    a = matmul(h, w_in_t, layer=layer, w_is_nk=True, n_out=MLA_IN_PAD, tn=512, out_dtype=F32)
    cqn, ckvn, krn, kr2 = mla_latents(a, qnorm_w, kvnorm_w, kr_w, rope_tables)
    q = mla_queries(cqn, w_uq, q_nope_w, q_rope_w2, rope_tables, layer=layer)
    ckv_all, kr2_all = ckvn.astype(BF16), kr2
    if ctx is not None:
        ckv_ctx, kr_ctx = ctx
        t = ckvn.shape[0] // batch
        kr2_ctx = jnp.concatenate([kr_ctx, kr_ctx], axis=-1).astype(BF16)
        ckv_all = jnp.concatenate([ckv_ctx.astype(BF16), ckv_all.reshape(batch, t, -1)], axis=1).reshape(-1, MLA_KV_LORA)
        kr2_all = jnp.concatenate([kr2_ctx, kr2.reshape(batch, t, -1)], axis=1).reshape(-1, LANES)
    k, v = mla_keys_values(ckv_all, kr2_all, w_ukv, k_nope_w, layer=layer, tm=1024 if ctx is None else 1536)
    if ctx is None:
        o = attention(q, k, v, batch=batch, heads_per_step=MLA_HEADS, tq=q.shape[0] // batch)
    else:
        o = attention(q, k, v, batch=batch, heads_per_step=8, tq=512)
    x = matmul(o, w_out, layer=layer, n_out=D_MODEL, tn=512, out_dtype=F32, epilogue="resid", res=x, mod=mod,
               gate_piece=2, **mod_kw)
    return x, ckvn, krn


def _mlp(x, mod, norm_w, w1, w2, *, layer, mod_kw):
    h = norm_modulate(x, norm_w[layer], mod, 3, **mod_kw)
    u = matmul(h, w1, layer=layer, n_out=D_FF, tn=512, out_dtype=BF16, epilogue="relu2", x_buffers=2)
    return matmul_ksplit_resid(u, w2, x, mod, layer=layer, gate_piece=5, **mod_kw)


def kernel(x_prompt, x_sample, state_mlstm_C, state_mlstm_n, state_mlstm_m, cache_mla_ckv, cache_mla_krope, c, c_ctx,
           norm1_w, norm2_w, mod_w, mod_b, mlp_w1, mlp_w2,
           mlstm_w_in, mlstm_gate_b, mlstm_hnorm_w, mlstm_w_out,
           mla_w_in, mla_qnorm_w, mla_kvnorm_w, mla_w_uq, mla_w_ukv, mla_q_norm_w, mla_k_norm_w, mla_w_out):
    bp, tp, d = x_prompt.shape
    bs, ts, _ = x_sample.shape
    xp = x_prompt.reshape(bp * tp, d)
    xs = x_sample.reshape(bs * ts, d)
    cond = jnp.concatenate([c_ctx[None, :], c, jnp.zeros((MOD_ROWS - 1 - bs, d), F32)], axis=0)
    mod = modulation_table(cond, mod_w, mod_b)
    mod0, mod1 = mod[0], mod[1]
    kw_p = dict(rows_per_mod=bp * tp, mod_base=0)
    kw_s = dict(rows_per_mod=ts, mod_base=1)

    wa = (jnp.swapaxes(mlstm_w_in, 1, 2), mlstm_gate_b, mlstm_hnorm_w, mlstm_w_out)
    hp = norm_modulate(xp, norm1_w[0], mod0, 0, **kw_p)
    hs = norm_modulate(xs, norm1_w[0], mod0, 0, **kw_s)
    xp, (c_new, n_new, m_new) = _mlstm_layer(hp, xp, mod0, *wa, layer=0, batch=bp, state=None, emit_state=True,
                                             mod_kw=kw_p)
    state = (state_mlstm_C[:, 0], state_mlstm_n[:, 0], state_mlstm_m[:, 0])
    xs, _ = _mlstm_layer(hs, xs, mod0, *wa, layer=0, batch=bs, state=state, emit_state=False, mod_kw=kw_s)
    xp = _mlp(xp, mod0, norm2_w, mlp_w1, mlp_w2, layer=0, mod_kw=kw_p)
    xs = _mlp(xs, mod0, norm2_w, mlp_w1, mlp_w2, layer=0, mod_kw=kw_s)

    qw, kw = mla_q_norm_w[0], mla_k_norm_w[0]
    pad = lambda w: jnp.concatenate([w, jnp.zeros_like(w)]).reshape(1, LANES)
    both_orders = lambda w: jnp.stack([w, jnp.roll(w, MLA_NOPE // 2)])
    wb = (
        jnp.swapaxes(mla_w_in, 1, 2), mla_qnorm_w[0], mla_kvnorm_w[0], mla_w_uq, mla_w_ukv,
        both_orders(qw[:MLA_NOPE]), jnp.concatenate([qw[MLA_NOPE:], qw[MLA_NOPE:]]).reshape(1, LANES),
        both_orders(kw[:MLA_NOPE]), pad(kw[MLA_NOPE:]), mla_w_out,
    )
    hp = norm_modulate(xp, norm1_w[1], mod1, 0, **kw_p)
    hs = norm_modulate(xs, norm1_w[1], mod1, 0, **kw_s)
    xp, ckv_p, kr_p = _mla_layer(hp, xp, mod1, wb, layer=0, batch=bp, ctx=None, rope_tables=None, mod_kw=kw_p)
    xs, _, _ = _mla_layer(hs, xs, mod1, wb, layer=0, batch=bs, ctx=(cache_mla_ckv[:, 0], cache_mla_krope[:, 0]),
                          rope_tables=_rope_tables(ts), mod_kw=kw_s)
    xp = _mlp(xp, mod1, norm2_w, mlp_w1, mlp_w2, layer=1, mod_kw=kw_p)
    xs = _mlp(xs, mod1, norm2_w, mlp_w1, mlp_w2, layer=1, mod_kw=kw_s)

    return (
        xp.reshape(bp, tp, d),
        xs.reshape(bs, ts, d),
        c_new.reshape(bp, 1, 2, MLSTM_HEADS, MLSTM_DK, MLSTM_DV),
        n_new.reshape(bp, 1, 2, MLSTM_HEADS, MLSTM_DK),
        m_new[..., 0, 0].reshape(bp, 1, 2, MLSTM_HEADS),
        ckv_p.reshape(bp, 1, tp, MLA_KV_LORA),
        kr_p[:, :MLA_ROPE].reshape(bp, 1, tp, MLA_ROPE),
    )
```
